```python
import jax, jax.numpy as jnp
from jax import lax
import numpy as np

D_MODEL = 2048
BATCH = 8
SEQ = 2048
DEPTH = 1
DEC_BATCH = 128
DEC_SEQ = 4
PAST_LEN = 2048
PAGE_SIZE = 128

N_META = 16
MIX = D_MODEL
SB_WIDTH = MIX // 2
SB_HEADS = 8
SB_HD = SB_WIDTH // SB_HEADS
SB_BIAS_INIT = -8.0
GLA_WIDTH = MIX - SB_WIDTH
GLA_HEADS = 4
GLA_KEY = GLA_WIDTH // 2
GLA_DK = GLA_KEY // GLA_HEADS
GLA_DV = GLA_WIDTH // GLA_HEADS
GLA_RANK = 16
GLA_TAU = 16.0
GLA_CHUNK = 64
Q_BLOCK = 128
EPS = 1e-6
SPLIT_SIZES = (SB_WIDTH, SB_WIDTH, SB_WIDTH, SB_WIDTH, GLA_KEY, GLA_KEY, GLA_WIDTH, GLA_WIDTH, GLA_RANK)
N_IN = sum(SPLIT_SIZES)

kernel_name = 'stick_breaking_gla_hymba_step'


def rms_norm(x, g):
    xf = x.astype(jnp.float32)
    var = jnp.mean(xf * xf, axis=-1, keepdims=True)
    return (xf * lax.rsqrt(var + EPS) * g.astype(jnp.float32)).astype(x.dtype)


def branch_inputs(h, w_in, w_alpha, b_alpha):
    B, L, _ = h.shape
    u = h @ w_in
    idx = [int(i) for i in np.cumsum(SPLIT_SIZES)[:-1]]
    sq, sk, sv, sg, gq, gk, gv, gg, ga = jnp.split(u, idx, axis=-1)
    sb = tuple(t.reshape(B, L, SB_HEADS, SB_HD) for t in (sq, sk, sv))
    to_bh = lambda t: t.reshape(B, L, GLA_HEADS, -1).transpose(0, 2, 1, 3)
    log_f = jax.nn.log_sigmoid((ga @ w_alpha + b_alpha).astype(jnp.float32)) / GLA_TAU
    gla = (to_bh(gq) * (GLA_DK ** -0.5), to_bh(gk), to_bh(gv), to_bh(log_f))
    return sb, sg, gla, gg


def branch_output(o_sb, sg, o_gla, gg, gla_norm_g, w_out):
    B, L = sg.shape[:2]
    o_gla = rms_norm(o_gla, gla_norm_g)
    o_gla = o_gla.transpose(0, 2, 1, 3).reshape(B, L, GLA_WIDTH)
    o_sb = o_sb.reshape(B, L, SB_WIDTH)
    mixed = jnp.concatenate([o_sb * jax.nn.silu(sg), o_gla * jax.nn.silu(gg)], axis=-1)
    return mixed @ w_out


def stick_break_weights(z, mask):
    sp = jnp.where(mask, jax.nn.softplus(z), 0.0)
    later = lax.cumsum(sp, axis=z.ndim - 1, reverse=True) - sp
    return jnp.where(mask, jnp.exp(jax.nn.log_sigmoid(z) - later), 0.0)


def sb_logits(q, k, bias):
    z = jnp.einsum('bqhd,bkhd->bhqk', q, k).astype(jnp.float32) * (SB_HD ** -0.5)
    return z + bias.astype(jnp.float32)[None, :, None, None]


def sb_attend(qb, qpos, k, v, kpos, bias):
    a = stick_break_weights(sb_logits(qb, k, bias), kpos[None, :] < qpos[:, None])
    return jnp.einsum('bhqk,bkhd->bqhd', a.astype(v.dtype), v)


def sb_prompt(q, k, v, bias):
    B, L = q.shape[:2]
    pos = jnp.arange(L)
    o_meta = sb_attend(q[:, :N_META], pos[:N_META], k[:, :N_META], v[:, :N_META], pos[:N_META], bias)
    nb = (L - N_META) // Q_BLOCK
    qb = q[:, N_META:].reshape(B, nb, Q_BLOCK, SB_HEADS, SB_HD).transpose(1, 0, 2, 3, 4)
    starts = N_META + jnp.arange(nb) * Q_BLOCK
    o_real = lax.map(lambda a: sb_attend(a[0], a[1] + jnp.arange(Q_BLOCK), k, v, pos, bias), (qb, starts))
    o_real = o_real.transpose(1, 0, 2, 3, 4).reshape(B, L - N_META, SB_HEADS, SB_HD)
    return jnp.concatenate([o_meta, o_real], axis=1)


def sb_sample(q, k_new, v_new, k_past, v_past, bias):
    P, T = k_past.shape[1], q.shape[1]
    z = jnp.concatenate([sb_logits(q, k_past, bias), sb_logits(q, k_new, bias)], axis=-1)
    mask = jnp.arange(P + T)[None, :] < (P + jnp.arange(T))[:, None]
    a = stick_break_weights(z, mask)
    return (jnp.einsum('bhqk,bkhd->bqhd', a[..., :P].astype(v_past.dtype), v_past)
            + jnp.einsum('bhqk,bkhd->bqhd', a[..., P:].astype(v_new.dtype), v_new))


def gla_chunk(S, q, k, v, g):
    C = q.shape[2]
    b = jnp.cumsum(g, axis=2)
    causal = jnp.tril(jnp.ones((C, C), dtype=bool))
    diff = b[:, :, :, None, :] - b[:, :, None, :, :]
    decay = jnp.where(causal[None, None, :, :, None], jnp.exp(jnp.minimum(diff, 0.0)), 0.0)
    attn = jnp.einsum('bhtk,bhsk,bhtsk->bhts', q, k, decay)
    o = jnp.einsum('bhts,bhsv->bhtv', attn, v) + jnp.einsum('bhtk,bhkv->bhtv', q * jnp.exp(b), S)
    b_last = b[:, :, -1:, :]
    S_new = (jnp.exp(b_last[:, :, 0, :])[..., None] * S
             + jnp.einsum('bhsk,bhsv->bhkv', k * jnp.exp(b_last - b), v))
    return S_new, o


def gla_prompt(q, k, v, g):
    B, H, L, _ = q.shape
    S0 = jnp.zeros((B, H, GLA_DK, GLA_DV), jnp.float32)
    S1, o_meta = gla_chunk(S0, q[:, :, :N_META], k[:, :, :N_META], v[:, :, :N_META], g[:, :, :N_META])
    nc = (L - N_META) // GLA_CHUNK
    to_chunks = lambda t: t[:, :, N_META:].reshape(B, H, nc, GLA_CHUNK, t.shape[-1]).transpose(2, 0, 1, 3, 4)
    S_end, o_real = lax.scan(lambda S, xs: gla_chunk(S, *xs), S1,
                             (to_chunks(q), to_chunks(k), to_chunks(v), to_chunks(g)))
    o_real = o_real.transpose(1, 2, 0, 3, 4).reshape(B, H, L - N_META, GLA_DV)
    return jnp.concatenate([o_meta, o_real], axis=2), S_end


def setup_inputs(seed: int = 0) -> dict:
    key = jax.random.key(seed)
    ks = jax.random.split(key, 16)
    n_pages = PAST_LEN // PAGE_SIZE
    n_pool = (5 * DEC_BATCH * n_pages) // 4
    nrm = lambda k, s, sc: jax.random.normal(k, s, jnp.float32) * sc
    page_table = jax.random.permutation(ks[0], n_pool)[:DEC_BATCH * n_pages].reshape(DEC_BATCH, n_pages).astype(jnp.int32)
    return {
        'x_prompt': nrm(ks[1], (BATCH, SEQ, D_MODEL), 1.0),
        'x_sample': nrm(ks[2], (DEC_BATCH, DEC_SEQ, D_MODEL), 1.0),
        'cache_k': nrm(ks[3], (DEPTH, n_pool, PAGE_SIZE, SB_HEADS, SB_HD), 1.0),
        'cache_v': nrm(ks[4], (DEPTH, n_pool, PAGE_SIZE, SB_HEADS, SB_HD), 1.0),
        'state_gla': nrm(ks[5], (DEPTH, DEC_BATCH, GLA_HEADS, GLA_DK, GLA_DV), 0.5),
        'page_table': page_table,
        'meta_tokens': nrm(ks[6], (N_META, D_MODEL), 1.0),
        'norm_pre_g': 1.0 + nrm(ks[7], (DEPTH, D_MODEL), 0.02),
        'w_in': nrm(ks[8], (DEPTH, D_MODEL, N_IN), D_MODEL ** -0.5),
        'sb_bias': SB_BIAS_INIT + nrm(ks[14], (DEPTH, SB_HEADS), 0.1),
        'w_alpha': nrm(ks[9], (DEPTH, GLA_RANK, GLA_KEY), GLA_RANK ** -0.5),
        'b_alpha': nrm(ks[10], (DEPTH, GLA_KEY), 0.1),
        'gla_norm_g': 1.0 + nrm(ks[11], (DEPTH, GLA_DV), 0.02),
        'w_out': nrm(ks[12], (DEPTH, MIX, D_MODEL), MIX ** -0.5),
        'norm_post_g': 1.0 + nrm(ks[13], (DEPTH, D_MODEL), 0.02),
    }


def reference(x_prompt, x_sample, cache_k, cache_v, state_gla, page_table, meta_tokens,
              norm_pre_g, w_in, sb_bias, w_alpha, b_alpha, gla_norm_g, w_out, norm_post_g):
    B = x_prompt.shape[0]
    meta = jnp.broadcast_to(meta_tokens[None].astype(x_prompt.dtype), (B, N_META, D_MODEL))
    x = jnp.concatenate([meta, x_prompt], axis=1)
    k_p, v_p, s_p = [], [], []
    for l in range(DEPTH):
        h = rms_norm(x, norm_pre_g[l])
        (q, k, v), sg, (gq, gk, gv, gf), gg = branch_inputs(h, w_in[l], w_alpha[l], b_alpha[l])
        o_sb = sb_prompt(q, k, v, sb_bias[l])
        o_gla, s_end = gla_prompt(gq, gk, gv, gf)
        x = x + rms_norm(branch_output(o_sb, sg, o_gla, gg, gla_norm_g[l], w_out[l]), norm_post_g[l])
        k_p.append(k)
        v_p.append(v)
        s_p.append(s_end)
    y_prompt = x[:, N_META:]

    DB = x_sample.shape[0]
    n_past = page_table.shape[1] * PAGE_SIZE
    xs = x_sample
    k_s, v_s, s_s = [], [], []
    for l in range(DEPTH):
        h = rms_norm(xs, norm_pre_g[l])
        (q, k, v), sg, (gq, gk, gv, gf), gg = branch_inputs(h, w_in[l], w_alpha[l], b_alpha[l])
        k_past = cache_k[l][page_table].reshape(DB, n_past, SB_HEADS, SB_HD)
        v_past = cache_v[l][page_table].reshape(DB, n_past, SB_HEADS, SB_HD)
        o_sb = sb_sample(q, k, v, k_past, v_past, sb_bias[l])
        s_new, o_gla = gla_chunk(state_gla[l], gq, gk, gv, gf)
        xs = xs + rms_norm(branch_output(o_sb, sg, o_gla, gg, gla_norm_g[l], w_out[l]), norm_post_g[l])
        k_s.append(k)
        v_s.append(v)
        s_s.append(s_new)
    y_sample = xs
    return (y_prompt, y_sample, jnp.stack(k_p), jnp.stack(v_p), jnp.stack(s_p),
            jnp.stack(k_s), jnp.stack(v_s), jnp.stack(s_s))
```

```python
import functools
import math

import jax
import jax.numpy as jnp
from jax import lax
from jax.experimental import pallas as pl
from jax.experimental.pallas import tpu as pltpu

F32 = jnp.float32
BF16 = jnp.bfloat16

N_META = 16
SB_HEADS = 8
SB_HD = 128
GLA_HEADS = 4
GLA_DK = 128
GLA_DV = 256
GLA_RANK = 16
GLA_TAU = 16.0
GLA_CHUNK = 64
PAGE_SIZE = 128
EPS = 1e-6

LANE = 128
SUBLANE = 8
VMEM_LIMIT = 56 * 1024 * 1024

SB_WIDTH = SB_HEADS * SB_HD
GLA_KEY = GLA_HEADS * GLA_DK
GLA_WIDTH = GLA_HEADS * GLA_DV
N_MAIN = 4 * SB_WIDTH + 2 * GLA_KEY + 2 * GLA_WIDTH
COL_SQ, COL_SK, COL_SV, COL_SG = 0, SB_WIDTH, 2 * SB_WIDTH, 3 * SB_WIDTH
COL_GQ = 4 * SB_WIDTH
COL_GK = COL_GQ + GLA_KEY
COL_GV = COL_GK + GLA_KEY
COL_GG = COL_GV + GLA_WIDTH


def _log1p_exp_neg_abs(z):
    return jnp.log1p(jnp.exp(-jnp.abs(z)))


def _log_sigmoid(z):
    return jnp.minimum(z, 0.0) - _log1p_exp_neg_abs(z)


def _split_bf16(x):
    hi = x.astype(BF16)
    lo = (x - hi.astype(F32)).astype(BF16)
    return hi, lo


def _dot(a, b):
    return jnp.dot(a, b, preferred_element_type=F32)


def _dot_nt(a, b):
    return lax.dot_general(a, b, (((1,), (1,)), ((), ())), preferred_element_type=F32)


def _dot_tn(a, b):
    return lax.dot_general(a, b, (((0,), (0,)), ((), ())), preferred_element_type=F32)


def _in_proj_kernel(x_ref, g_ref, w_ref, wga_ref, wal_ref, bal_ref, u_ref, lf_ref, h_scr):
    @pl.when(pl.program_id(1) == 0)
    def _():
        x = x_ref[...]
        var = jnp.mean(x * x, axis=-1, keepdims=True)
        h = (x * lax.rsqrt(var + EPS) * g_ref[...]).astype(BF16)
        h_scr[...] = h
        ga = _dot(h, wga_ref[...])
        pre = _dot(ga.astype(BF16), wal_ref[...]) + bal_ref[...]
        lf_ref[...] = _log_sigmoid(pre) / GLA_TAU

    u_ref[...] = _dot(h_scr[...], w_ref[...])


def _in_proj(x_all, g_pre, w_main, w_ga, w_al, b_al, *, tm, tn):
    m, d = x_all.shape
    n = w_main.shape[1]
    assert m % tm == 0 and n % tn == 0
    return pl.pallas_call(
        _in_proj_kernel,
        grid=(m // tm, n // tn),
        in_specs=[
            pl.BlockSpec((tm, d), lambda i, j: (i, 0)),
            pl.BlockSpec((1, d), lambda i, j: (0, 0)),
            pl.BlockSpec((d, tn), lambda i, j: (0, j)),
            pl.BlockSpec((d, LANE), lambda i, j: (0, 0)),
            pl.BlockSpec((LANE, GLA_KEY), lambda i, j: (0, 0)),
            pl.BlockSpec((1, GLA_KEY), lambda i, j: (0, 0)),
        ],
        out_specs=[
            pl.BlockSpec((tm, tn), lambda i, j: (i, j)),
            pl.BlockSpec((tm, GLA_KEY), lambda i, j: (i, 0)),
        ],
        out_shape=[
            jax.ShapeDtypeStruct((m, n), F32),
            jax.ShapeDtypeStruct((m, GLA_KEY), F32),
        ],
        scratch_shapes=[pltpu.VMEM((tm, d), BF16)],
        compiler_params=pltpu.CompilerParams(
            dimension_semantics=("arbitrary", "arbitrary"), vmem_limit_bytes=VMEM_LIMIT),
        name="in_proj",
    )(x_all, g_pre, w_main, w_ga, w_al, b_al)


def _sb_block(z_raw, bias, mask, v_bf, u_mat, carry):
    z = z_raw * (SB_HD ** -0.5) + bias
    t = _log1p_exp_neg_abs(z)
    sp = jnp.maximum(z, 0.0) + t
    ls = jnp.minimum(z, 0.0) - t
    if mask is not None:
        sp = jnp.where(mask, sp, 0.0)
    hi, lo = _split_bf16(sp)
    later = _dot(hi, u_mat) + _dot(lo, u_mat) + carry
    a = jnp.exp(ls - later)
    if mask is not None:
        a = jnp.where(mask, a, 0.0)
    pv = _dot(a.astype(BF16), v_bf)
    return pv, carry + jnp.sum(sp, axis=-1, keepdims=True)


def _sb_prompt_kernel(bias_ref, q_ref, k_ref, v_ref, km_ref, vm_ref, u_ref, o_ref, *, tq):
    h = pl.program_id(1)
    i = pl.program_id(2)
    bias = bias_ref[h]
    q = q_ref[...].astype(BF16)
    row = lax.broadcasted_iota(jnp.int32, (tq, tq), 0)
    col = lax.broadcasted_iota(jnp.int32, (tq, tq), 1)
    tri = col < row
    u_mat = u_ref[...]

    def body(it, state):
        acc, carry = state
        start = pl.multiple_of((i - it) * tq, tq)
        k = k_ref[pl.ds(start, tq), :].astype(BF16)
        v = v_ref[pl.ds(start, tq), :].astype(BF16)
        mask = jnp.logical_or(tri, it > 0)
        pv, carry = _sb_block(_dot_nt(q, k), bias, mask, v, u_mat, carry)
        return acc + pv, carry

    acc0 = jnp.zeros((tq, SB_HD), F32)
    c0 = jnp.zeros((tq, 1), F32)
    acc, carry = lax.fori_loop(0, i + 1, body, (acc0, c0))

    km = km_ref[...].astype(BF16)
    vm = vm_ref[...].astype(BF16)
    mmask = lax.broadcasted_iota(jnp.int32, (tq, LANE), 1) < N_META
    pv, _ = _sb_block(_dot_nt(q, km), bias, mmask, vm, u_ref[:LANE, :LANE], carry)
    o_ref[...] = acc + pv


def _sb_prompt(u, sb_bias, u_tri, *, n_batch, seq, meta_row, tq):
    nq = seq // tq
    qb = COL_SQ // SB_HD
    kb = COL_SK // SB_HD
    vb = COL_SV // SB_HD
    mrow = meta_row // LANE
    kern = functools.partial(_sb_prompt_kernel, tq=tq)
    return pl.pallas_call(
        kern,
        grid=(n_batch, SB_HEADS, nq),
        in_specs=[
            pl.BlockSpec(memory_space=pltpu.SMEM),
            pl.BlockSpec((tq, SB_HD), lambda b, h, i: (b * nq + i, qb + h)),
            pl.BlockSpec((seq, SB_HD), lambda b, h, i: (b, kb + h)),
            pl.BlockSpec((seq, SB_HD), lambda b, h, i: (b, vb + h)),
            pl.BlockSpec((LANE, SB_HD), lambda b, h, i: (mrow, kb + h)),
            pl.BlockSpec((LANE, SB_HD), lambda b, h, i: (mrow, vb + h)),
            pl.BlockSpec((tq, tq), lambda b, h, i: (0, 0)),
        ],
        out_specs=pl.BlockSpec((tq, SB_HD), lambda b, h, i: (b * nq + i, h)),
        out_shape=jax.ShapeDtypeStruct((n_batch * seq, SB_WIDTH), F32),
        compiler_params=pltpu.CompilerParams(
            dimension_semantics=("arbitrary", "arbitrary", "arbitrary"), vmem_limit_bytes=VMEM_LIMIT),
        name="sb_prompt",
    )(sb_bias, u, u, u, u, u, u_tri)


def _gla_chunk(s_prev, q, k, v, g, l_tri, ones_c):
    c = q.shape[0]
    g_hi, g_lo = _split_bf16(g)
    b = _dot(l_tri, g_hi) + _dot(l_tri, g_lo)
    qe = (q * jnp.exp(b)).astype(BF16)
    ke = (k * jnp.exp(-b)).astype(BF16)
    v_bf = v.astype(BF16)
    attn = _dot_nt(qe, ke)
    row = lax.broadcasted_iota(jnp.int32, (c, c), 0)
    col = lax.broadcasted_iota(jnp.int32, (c, c), 1)
    attn = jnp.where(col <= row, attn, 0.0)
    o = _dot(attn.astype(BF16), v_bf) + _dot(qe, s_prev.astype(BF16))
    b_last = b[c - 1:c, :]
    kd = (k * jnp.exp(b_last - b)).astype(BF16)
    dec = jnp.exp(_dot_tn(g_hi, ones_c) + _dot_tn(g_lo, ones_c))
    dec2 = jnp.concatenate([dec] * (GLA_DV // LANE), axis=1)
    s_new = dec2 * s_prev + _dot_tn(kd, v_bf)
    return s_new, o


def _gla_prompt_kernel(q_ref, k_ref, v_ref, g_ref, qm_ref, km_ref, vm_ref, gm_ref,
                       lt_ref, ones_ref, o_ref, s_ref, s_scr, *, n_chunks, chunk):
    scale = GLA_DK ** -0.5
    lt = lt_ref[...]
    ones = ones_ref[...]
    s0 = jnp.zeros((GLA_DK, GLA_DV), F32)
    s1, _ = _gla_chunk(s0, qm_ref[...] * scale, km_ref[...], vm_ref[...], gm_ref[...],
                       lt[:N_META, :N_META], ones[:N_META])
    s_scr[...] = s1

    def body(ci, _):
        start = pl.multiple_of(ci * chunk, chunk)
        sl = pl.ds(start, chunk)
        s_new, o = _gla_chunk(s_scr[...], q_ref[sl, :] * scale, k_ref[sl, :], v_ref[sl, :],
                              g_ref[sl, :], lt, ones)
        s_scr[...] = s_new
        o_ref[sl, :] = o
        return 0

    lax.fori_loop(0, n_chunks, body, 0)
    s_ref[0, 0] = s_scr[...]


def _gla_prompt(u, logf, l_tri, ones_c, *, n_batch, seq, meta_row):
    qb = COL_GQ // GLA_DK
    kb = COL_GK // GLA_DK
    vb = COL_GV // GLA_DV
    mrow = meta_row // N_META
    kern = functools.partial(_gla_prompt_kernel, n_chunks=seq // GLA_CHUNK, chunk=GLA_CHUNK)
    return pl.pallas_call(
        kern,
        grid=(n_batch, GLA_HEADS),
        in_specs=[
            pl.BlockSpec((seq, GLA_DK), lambda b, h: (b, qb + h)),
            pl.BlockSpec((seq, GLA_DK), lambda b, h: (b, kb + h)),
            pl.BlockSpec((seq, GLA_DV), lambda b, h: (b, vb + h)),
            pl.BlockSpec((seq, GLA_DK), lambda b, h: (b, h)),
            pl.BlockSpec((N_META, GLA_DK), lambda b, h: (mrow, qb + h)),
            pl.BlockSpec((N_META, GLA_DK), lambda b, h: (mrow, kb + h)),
            pl.BlockSpec((N_META, GLA_DV), lambda b, h: (mrow, vb + h)),
            pl.BlockSpec((N_META, GLA_DK), lambda b, h: (mrow, h)),
            pl.BlockSpec((GLA_CHUNK, GLA_CHUNK), lambda b, h: (0, 0)),
            pl.BlockSpec((GLA_CHUNK, LANE), lambda b, h: (0, 0)),
        ],
        out_specs=[
            pl.BlockSpec((seq, GLA_DV), lambda b, h: (b, h)),
            pl.BlockSpec((1, 1, GLA_DK, GLA_DV), lambda b, h: (b, h, 0, 0)),
        ],
        out_shape=[
            jax.ShapeDtypeStruct((n_batch * seq, GLA_WIDTH), F32),
            jax.ShapeDtypeStruct((n_batch, GLA_HEADS, GLA_DK, GLA_DV), F32),
        ],
        scratch_shapes=[pltpu.VMEM((GLA_DK, GLA_DV), F32)],
        compiler_params=pltpu.CompilerParams(
            dimension_semantics=("arbitrary", "arbitrary"), vmem_limit_bytes=VMEM_LIMIT),
        name="gla_prompt",
    )(u, u, u, logf, u, u, u, logf, l_tri, ones_c)


def _gla_sample_kernel(q_ref, k_ref, v_ref, g_ref, s_in_ref, lt_ref, ones_ref, o_ref, s_out_ref):
    scale = GLA_DK ** -0.5
    lt = lt_ref[...]
    ones = ones_ref[...]
    outs = []
    for h in range(GLA_HEADS):
        ks = slice(h * GLA_DK, (h + 1) * GLA_DK)
        vs = slice(h * GLA_DV, (h + 1) * GLA_DV)
        s_new, o = _gla_chunk(s_in_ref[0, h], q_ref[0, :, ks] * scale, k_ref[0, :, ks],
                              v_ref[0, :, vs], g_ref[0, :, ks], lt, ones)
        s_out_ref[0, h] = s_new
        outs.append(o)
    o_ref[0] = jnp.concatenate(outs, axis=1)


def _gla_sample(gq, gk, gv, gf, state, l_tri, ones_c):
    db, rows, _ = gq.shape
    return pl.pallas_call(
        _gla_sample_kernel,
        grid=(db,),
        in_specs=[
            pl.BlockSpec((1, rows, GLA_KEY), lambda b: (b, 0, 0)),
            pl.BlockSpec((1, rows, GLA_KEY), lambda b: (b, 0, 0)),
            pl.BlockSpec((1, rows, GLA_WIDTH), lambda b: (b, 0, 0)),
            pl.BlockSpec((1, rows, GLA_KEY), lambda b: (b, 0, 0)),
            pl.BlockSpec((1, GLA_HEADS, GLA_DK, GLA_DV), lambda b: (b, 0, 0, 0)),
            pl.BlockSpec((rows, rows), lambda b: (0, 0)),
            pl.BlockSpec((rows, LANE), lambda b: (0, 0)),
        ],
        out_specs=[
            pl.BlockSpec((1, rows, GLA_WIDTH), lambda b: (b, 0, 0)),
            pl.BlockSpec((1, GLA_HEADS, GLA_DK, GLA_DV), lambda b: (b, 0, 0, 0)),
        ],
        out_shape=[
            jax.ShapeDtypeStruct((db, rows, GLA_WIDTH), F32),
            jax.ShapeDtypeStruct((db, GLA_HEADS, GLA_DK, GLA_DV), F32),
        ],
        compiler_params=pltpu.CompilerParams(
            dimension_semantics=("arbitrary",), vmem_limit_bytes=VMEM_LIMIT),
        name="gla_sample",
    )(gq, gk, gv, gf, state, l_tri, ones_c)


PAGES_PER_STEP = 8


def _sb_sample_kernel(pt_ref, q_ref, kn_ref, vn_ref, *refs, dec_seq):
    del pt_ref
    npg = PAGES_PER_STEP
    k_refs = refs[:npg]
    v_refs = refs[npg:2 * npg]
    u_ref, bias_ref, o_ref, qbd_scr, k2d_scr, v2d_scr, acc_scr, c_scr = refs[2 * npg:]
    g = pl.program_id(1)
    n_rows = dec_seq * SB_HEADS
    u_mat = u_ref[...]
    bias = bias_ref[...][:, :1]

    @pl.when(g == 0)
    def _():
        head_of_col = lax.broadcasted_iota(jnp.int32, (SB_HEADS, SB_WIDTH), 1) // SB_HD
        head_of_row = lax.broadcasted_iota(jnp.int32, (SB_HEADS, SB_WIDTH), 0)
        hmask = head_of_col == head_of_row
        qbd = jnp.concatenate(
            [jnp.where(hmask, jnp.broadcast_to(q_ref[0, t:t + 1, :], (SB_HEADS, SB_WIDTH)), 0.0)
             for t in range(dec_seq)], axis=0)
        qbd_scr[...] = qbd.astype(BF16)
        lane = lax.broadcasted_iota(jnp.int32, (n_rows, LANE), 1)
        t_of_row = lax.broadcasted_iota(jnp.int32, (n_rows, LANE), 0) // SB_HEADS
        z_new = jnp.zeros((n_rows, LANE), F32)
        for t in range(dec_seq):
            zc = jnp.sum(qbd * kn_ref[0, t:t + 1, :], axis=-1, keepdims=True)
            z_new = jnp.where(lane == t, zc, z_new)
        z = z_new * (SB_HD ** -0.5) + bias
        mask = lane < t_of_row
        tt = _log1p_exp_neg_abs(z)
        sp = jnp.where(mask, jnp.maximum(z, 0.0) + tt, 0.0)
        ls = jnp.minimum(z, 0.0) - tt
        hi, lo = _split_bf16(sp)
        later = _dot(hi, u_mat) + _dot(lo, u_mat)
        a = jnp.where(mask, jnp.exp(ls - later), 0.0)
        acc = jnp.zeros((n_rows, SB_WIDTH), F32)
        for t in range(dec_seq):
            acc = acc + a[:, t:t + 1] * vn_ref[0, t:t + 1, :]
        acc_scr[...] = acc
        c_scr[...] = jnp.broadcast_to(jnp.sum(sp, axis=-1, keepdims=True), c_scr.shape)

    for s in range(npg):
        for h in range(SB_HEADS):
            rows = slice(s * PAGE_SIZE, (s + 1) * PAGE_SIZE)
            cols = slice(h * SB_HD, (h + 1) * SB_HD)
            k2d_scr[rows, cols] = k_refs[s][pl.ds(h, PAGE_SIZE, stride=SB_HEADS), :].astype(BF16)
            v2d_scr[rows, cols] = v_refs[s][pl.ds(h, PAGE_SIZE, stride=SB_HEADS), :].astype(BF16)

    z = _dot_nt(qbd_scr[...], k2d_scr[...]) * (SB_HD ** -0.5) + bias
    tt = _log1p_exp_neg_abs(z)
    sp = jnp.maximum(z, 0.0) + tt
    ls = jnp.minimum(z, 0.0) - tt
    blocks = [sp[:, j * LANE:(j + 1) * LANE] for j in range(npg)]
    hi, lo = _split_bf16(jnp.concatenate(blocks, axis=0))
    loc = _dot(hi, u_mat) + _dot(lo, u_mat)
    carry = c_scr[...][:, :1]
    laters = [None] * npg
    for j in range(npg - 1, -1, -1):
        laters[j] = loc[j * n_rows:(j + 1) * n_rows] + carry
        carry = carry + jnp.sum(blocks[j], axis=-1, keepdims=True)
    a = jnp.exp(ls - jnp.concatenate(laters, axis=1))
    acc_scr[...] += _dot(a.astype(BF16), v2d_scr[...])
    c_scr[...] = jnp.broadcast_to(carry, c_scr.shape)

    @pl.when(g == pl.num_programs(1) - 1)
    def _():
        head_of_col = lax.broadcasted_iota(jnp.int32, (SB_HEADS, SB_WIDTH), 1) // SB_HD
        head_of_row = lax.broadcasted_iota(jnp.int32, (SB_HEADS, SB_WIDTH), 0)
        hmask = head_of_col == head_of_row
        for t in range(dec_seq):
            blk = acc_scr[t * SB_HEADS:(t + 1) * SB_HEADS, :]
            o_ref[0, t:t + 1, :] = jnp.sum(jnp.where(hmask, blk, 0.0), axis=0, keepdims=True)


def _sb_sample(page_table, q, k_new, v_new, cache_k, cache_v, u_tri, bias_rows):
    db, dec_seq, _ = q.shape
    n_pages = page_table.shape[1]
    npg = PAGES_PER_STEP
    assert n_pages % npg == 0
    n_groups = n_pages // npg
    n_rows = dec_seq * SB_HEADS
    page_rows = PAGE_SIZE * SB_HEADS
    pt_flat = page_table.reshape(-1)

    def page_spec(s):
        return pl.BlockSpec(
            (page_rows, SB_HD),
            lambda b, g, pt: (pt[b * n_pages + (n_groups - 1 - g) * npg + s], 0))

    small = pl.BlockSpec((1, dec_seq, SB_WIDTH), lambda b, g, pt: (b, 0, 0))
    grid_spec = pltpu.PrefetchScalarGridSpec(
        num_scalar_prefetch=1,
        grid=(db, n_groups),
        in_specs=[small, small, small]
        + [page_spec(s) for s in range(npg)]
        + [page_spec(s) for s in range(npg)]
        + [pl.BlockSpec((LANE, LANE), lambda b, g, pt: (0, 0)),
           pl.BlockSpec((n_rows, LANE), lambda b, g, pt: (0, 0))],
        out_specs=pl.BlockSpec((1, dec_seq, SB_WIDTH), lambda b, g, pt: (b, 0, 0)),
        scratch_shapes=[
            pltpu.VMEM((n_rows, SB_WIDTH), BF16),
            pltpu.VMEM((npg * PAGE_SIZE, SB_WIDTH), BF16),
            pltpu.VMEM((npg * PAGE_SIZE, SB_WIDTH), BF16),
            pltpu.VMEM((n_rows, SB_WIDTH), F32),
            pltpu.VMEM((n_rows, LANE), F32),
        ],
    )
    kern = functools.partial(_sb_sample_kernel, dec_seq=dec_seq)
    return pl.pallas_call(
        kern,
        grid_spec=grid_spec,
        out_shape=jax.ShapeDtypeStruct((db, dec_seq, SB_WIDTH), F32),
        compiler_params=pltpu.CompilerParams(
            dimension_semantics=("arbitrary", "arbitrary"), vmem_limit_bytes=VMEM_LIMIT),
        name="sb_sample",
    )(pt_flat, q, k_new, v_new, *([cache_k] * npg), *([cache_v] * npg), u_tri, bias_rows)


def _out_proj_kernel(osb_ref, sg_ref, ogla_ref, gg_ref, x_ref, gng_ref, w_ref, png_ref, y_ref):
    m1 = osb_ref[...] * jax.nn.silu(sg_ref[...])
    og = ogla_ref[...]
    gng = gng_ref[...]
    normed = []
    for h in range(GLA_HEADS):
        blk = og[:, h * GLA_DV:(h + 1) * GLA_DV]
        var = jnp.mean(blk * blk, axis=-1, keepdims=True)
        normed.append(blk * lax.rsqrt(var + EPS) * gng)
    m2 = jnp.concatenate(normed, axis=1) * jax.nn.silu(gg_ref[...])
    mixed = jnp.concatenate([m1, m2], axis=1).astype(BF16)
    br = _dot(mixed, w_ref[...])
    var = jnp.mean(br * br, axis=-1, keepdims=True)
    y_ref[...] = x_ref[...] + br * lax.rsqrt(var + EPS) * png_ref[...]


def _out_proj(o_sb, u, o_gla, x, gng, w_out, png, *, row_off, tm):
    m, d = x.shape
    assert m % tm == 0 and row_off % tm == 0
    off = row_off // tm
    sgb = COL_SG // SB_WIDTH
    ggb = COL_GG // GLA_WIDTH
    mix = SB_WIDTH + GLA_WIDTH
    return pl.pallas_call(
        _out_proj_kernel,
        grid=(m // tm,),
        in_specs=[
            pl.BlockSpec((tm, SB_WIDTH), lambda i: (i, 0)),
            pl.BlockSpec((tm, SB_WIDTH), lambda i: (i + off, sgb)),
            pl.BlockSpec((tm, GLA_WIDTH), lambda i: (i, 0)),
            pl.BlockSpec((tm, GLA_WIDTH), lambda i: (i + off, ggb)),
            pl.BlockSpec((tm, d), lambda i: (i, 0)),
            pl.BlockSpec((1, GLA_DV), lambda i: (0, 0)),
            pl.BlockSpec((mix, d), lambda i: (0, 0)),
            pl.BlockSpec((1, d), lambda i: (0, 0)),
        ],
        out_specs=pl.BlockSpec((tm, d), lambda i: (i, 0)),
        out_shape=jax.ShapeDtypeStruct((m, d), F32),
        compiler_params=pltpu.CompilerParams(
            dimension_semantics=("arbitrary",), vmem_limit_bytes=VMEM_LIMIT),
        name="out_proj",
    )(o_sb, u, o_gla, u, x, gng, w_out, png)


def _tri(n, *, strict):
    r = jnp.arange(n)[:, None]
    c = jnp.arange(n)[None, :]
    return ((c < r) if strict else (c <= r)).astype(BF16)


def _largest_tile(m, cap):
    for t in range(min(cap, m) // SUBLANE * SUBLANE, 0, -SUBLANE):
        if m % t == 0:
            return t
    raise ValueError(f"no sublane-aligned tile divides {m}")


def kernel(x_prompt, x_sample, cache_k, cache_v, state_gla, page_table, meta_tokens,
           norm_pre_g, w_in, sb_bias, w_alpha, b_alpha, gla_norm_g, w_out, norm_post_g):
    n_batch, seq, d = x_prompt.shape
    db, dec_seq, _ = x_sample.shape
    depth = w_in.shape[0]
    assert depth == 1
    n_prompt = n_batch * seq
    n_sample = db * dec_seq
    meta_row = n_prompt + n_sample

    meta_pad = jnp.zeros((LANE - N_META, d), F32)
    x_all = jnp.concatenate(
        [x_prompt.reshape(n_prompt, d), x_sample.reshape(n_sample, d), meta_tokens, meta_pad], axis=0)

    w_main = w_in[0, :, :N_MAIN].astype(BF16)
    w_ga = jnp.pad(w_in[0, :, N_MAIN:], ((0, 0), (0, LANE - GLA_RANK))).astype(BF16)
    w_al = jnp.pad(w_alpha[0], ((0, LANE - GLA_RANK), (0, 0))).astype(BF16)
    u, logf = _in_proj(x_all, norm_pre_g, w_main, w_ga, w_al, b_alpha,
                       tm=_largest_tile(x_all.shape[0], 1024), tn=512)
    tm_out = _largest_tile(math.gcd(n_prompt, n_sample), 256)

    tq = 256
    o_sb_p = _sb_prompt(u, sb_bias[0], _tri(tq, strict=True),
                        n_batch=n_batch, seq=seq, meta_row=meta_row, tq=tq)
    l_tri = _tri(GLA_CHUNK, strict=False)
    ones_c = jnp.ones((GLA_CHUNK, LANE), BF16)
    o_gla_p, s_end = _gla_prompt(u, logf, l_tri, ones_c, n_batch=n_batch, seq=seq, meta_row=meta_row)
    w_out_bf = w_out[0].astype(BF16)
    y_prompt = _out_proj(o_sb_p, u, o_gla_p, x_prompt.reshape(n_prompt, d), gla_norm_g, w_out_bf,
                         norm_post_g, row_off=0, tm=tm_out).reshape(n_batch, seq, d)

    us = u[n_prompt:n_prompt + n_sample]
    lfs = logf[n_prompt:n_prompt + n_sample]
    q_s = us[:, COL_SQ:COL_SQ + SB_WIDTH].reshape(db, dec_seq, SB_WIDTH)
    k_s = us[:, COL_SK:COL_SK + SB_WIDTH].reshape(db, dec_seq, SB_WIDTH)
    v_s = us[:, COL_SV:COL_SV + SB_WIDTH].reshape(db, dec_seq, SB_WIDTH)
    n_pool = cache_k.shape[1]
    ck = cache_k.reshape(n_pool * PAGE_SIZE * SB_HEADS, SB_HD)
    cv = cache_v.reshape(n_pool * PAGE_SIZE * SB_HEADS, SB_HD)
    bias_rows = jnp.broadcast_to(jnp.tile(sb_bias[0], dec_seq)[:, None], (dec_seq * SB_HEADS, LANE))
    o_sb_s = _sb_sample(page_table, q_s, k_s, v_s, ck, cv, _tri(LANE, strict=True), bias_rows)

    rows = 2 * SUBLANE
    pad_rows = lambda t: jnp.pad(t.reshape(db, dec_seq, -1), ((0, 0), (0, rows - dec_seq), (0, 0)))
    o_gla_s, s_new = _gla_sample(
        pad_rows(us[:, COL_GQ:COL_GQ + GLA_KEY]), pad_rows(us[:, COL_GK:COL_GK + GLA_KEY]),
        pad_rows(us[:, COL_GV:COL_GV + GLA_WIDTH]), pad_rows(lfs), state_gla[0],
        _tri(rows, strict=False), jnp.ones((rows, LANE), BF16))
    y_sample = _out_proj(o_sb_s.reshape(n_sample, SB_WIDTH), u, o_gla_s[:, :dec_seq].reshape(n_sample, GLA_WIDTH),
                         x_sample.reshape(n_sample, d), gla_norm_g, w_out_bf, norm_post_g,
                         row_off=n_prompt, tm=tm_out).reshape(db, dec_seq, d)

    um = u[meta_row:meta_row + N_META]
    def with_meta(col):
        real = u[:n_prompt, col:col + SB_WIDTH].reshape(n_batch, seq, SB_HEADS, SB_HD)
        meta = jnp.broadcast_to(um[:, col:col + SB_WIDTH].reshape(1, N_META, SB_HEADS, SB_HD),
                                (n_batch, N_META, SB_HEADS, SB_HD))
        return jnp.concatenate([meta, real], axis=1)[None]
    new_k_prompt = with_meta(COL_SK)
    new_v_prompt = with_meta(COL_SV)
    new_k_sample = k_s.reshape(1, db, dec_seq, SB_HEADS, SB_HD)
    new_v_sample = v_s.reshape(1, db, dec_seq, SB_HEADS, SB_HD)
    return (y_prompt, y_sample, new_k_prompt, new_v_prompt, s_end[None],
            new_k_sample, new_v_sample, s_new[None])
```

```python
import functools
import math

import jax
import jax.numpy as jnp
from jax import lax
from jax.experimental import pallas as pl
from jax.experimental.pallas import tpu as pltpu

F32 = jnp.float32
BF16 = jnp.bfloat16

N_META = 16
SB_HEADS = 8
SB_HD = 128
GLA_HEADS = 4
GLA_DK = 128
GLA_DV = 256
GLA_RANK = 16
GLA_TAU = 16.0
PAGE_SIZE = 128
EPS = 1e-6

LANE = 128
SUBLANE = 8
VMEM_LIMIT = 56 * 1024 * 1024
LOG2E = math.log2(math.e)

SB_WIDTH = SB_HEADS * SB_HD
GLA_KEY = GLA_HEADS * GLA_DK
GLA_WIDTH = GLA_HEADS * GLA_DV
N_MAIN = 4 * SB_WIDTH + 2 * GLA_KEY + 2 * GLA_WIDTH
COL_SQ, COL_SK, COL_SV, COL_SG = 0, SB_WIDTH, 2 * SB_WIDTH, 3 * SB_WIDTH
COL_GQ = 4 * SB_WIDTH
COL_GK = COL_GQ + GLA_KEY
COL_GV = COL_GK + GLA_KEY
COL_GG = COL_GV + GLA_WIDTH

GLA_STEP = 2 * SUBLANE
assert N_META == GLA_STEP


def _log_sigmoid(z):
    return jnp.minimum(z, 0.0) - jnp.log1p(jnp.exp(-jnp.abs(z)))


def _split_bf16(x):
    hi = x.astype(BF16)
    lo = (x - hi.astype(F32)).astype(BF16)
    return hi, lo


def _dot(a, b):
    return jnp.dot(a, b, preferred_element_type=F32)


def _dot_nt(a, b):
    return lax.dot_general(a, b, (((1,), (1,)), ((), ())), preferred_element_type=F32)


def _dot_tn(a, b):
    return lax.dot_general(a, b, (((0,), (0,)), ((), ())), preferred_element_type=F32)


def _in_proj_kernel(x_ref, g_ref, w_ref, wga_ref, wal_ref, bal_ref, seg_ref, u_ref, lf_ref, h_scr):
    @pl.when(pl.program_id(1) == 0)
    def _():
        x = x_ref[...]
        var = jnp.mean(x * x, axis=-1, keepdims=True)
        h = (x * lax.rsqrt(var + EPS) * g_ref[...]).astype(BF16)
        h_scr[...] = h
        ga = _dot(h, wga_ref[...])
        pre = _dot(ga.astype(BF16), wal_ref[...]) + bal_ref[...]
        g_hi, g_lo = _split_bf16(_log_sigmoid(pre) / GLA_TAU)
        seg = seg_ref[...]
        for r in range(x.shape[0] // LANE):
            rows = slice(r * LANE, (r + 1) * LANE)
            lf_ref[rows, :] = _dot(seg, g_hi[rows]) + _dot(seg, g_lo[rows])

    u_ref[...] = _dot(h_scr[...], w_ref[...])


def _in_proj(x, g_pre, w_main, w_ga, w_al, b_al, seg, *, tm, tn):
    m, d = x.shape
    n = w_main.shape[1]
    assert m % tm == 0 and n % tn == 0 and tm % LANE == 0
    return pl.pallas_call(
        _in_proj_kernel,
        grid=(m // tm, n // tn),
        in_specs=[
            pl.BlockSpec((tm, d), lambda i, j: (i, 0)),
            pl.BlockSpec((1, d), lambda i, j: (0, 0)),
            pl.BlockSpec((d, tn), lambda i, j: (0, j)),
            pl.BlockSpec((d, LANE), lambda i, j: (0, 0)),
            pl.BlockSpec((LANE, GLA_KEY), lambda i, j: (0, 0)),
            pl.BlockSpec((1, GLA_KEY), lambda i, j: (0, 0)),
            pl.BlockSpec((LANE, LANE), lambda i, j: (0, 0)),
        ],
        out_specs=[
            pl.BlockSpec((tm, tn), lambda i, j: (i, j)),
            pl.BlockSpec((tm, GLA_KEY), lambda i, j: (i, 0)),
        ],
        out_shape=[
            jax.ShapeDtypeStruct((m, n), F32),
            jax.ShapeDtypeStruct((m, GLA_KEY), F32),
        ],
        scratch_shapes=[pltpu.VMEM((tm, d), BF16)],
        compiler_params=pltpu.CompilerParams(
            dimension_semantics=("arbitrary", "arbitrary"), vmem_limit_bytes=VMEM_LIMIT),
        name="in_proj",
    )(x, g_pre, w_main, w_ga, w_al, b_al, seg)


SB_ZSCALE2 = (SB_HD ** -0.5) * LOG2E


def _softplus2(z_raw, zbias2):
    z2 = z_raw * SB_ZSCALE2 + zbias2
    t2 = jnp.log(1.0 + jnp.exp2(-jnp.abs(z2))) * LOG2E
    sp2 = jnp.maximum(z2, 0.0) + t2
    return sp2, z2 - sp2


def _tri_cumsum(sp2, u2):
    hi, lo = _split_bf16(sp2)
    return _dot(jnp.concatenate([hi, lo], axis=1), u2)


def _sb_scores(z_raw, zbias2, mask, u2):
    sp2, ls2 = _softplus2(z_raw, zbias2)
    if mask is not None:
        sp2 = jnp.where(mask, sp2, 0.0)
    subs = [sp2[:, j * LANE:(j + 1) * LANE] for j in range(sp2.shape[1] // LANE)]
    loc = _tri_cumsum(jnp.concatenate(subs, axis=0), u2)
    return ls2, loc, [jnp.sum(s, axis=-1, keepdims=True) for s in subs]


def _sb_weights(scores, mask, carry):
    ls2, loc, sums = scores
    rows = ls2.shape[0]
    later = [None] * len(sums)
    for j in range(len(sums) - 1, -1, -1):
        later[j] = loc[j * rows:(j + 1) * rows] + carry
        carry = carry + sums[j]
    a = jnp.exp2(ls2 - jnp.concatenate(later, axis=1))
    if mask is not None:
        a = jnp.where(mask, a, 0.0)
    return a.astype(BF16), carry


def _sb_streams(streams, u2, scores_only=()):
    zs = [[_dot_nt(q, k) for k, _, _ in blocks] for q, _, blocks, _, _ in streams]
    zx = [_dot_nt(q, k) for q, _, k, _ in scores_only]
    scores = [[_sb_scores(z, zb, m, u2) for z, (_, _, m) in zip(zrow, blocks)]
              for zrow, (_, zb, blocks, _, _) in zip(zs, streams)]
    extra = [_sb_scores(z, zb, m, u2) for z, (_, zb, _, m) in zip(zx, scores_only)]
    out = []
    for (_, _, blocks, acc, carry), srow in zip(streams, scores):
        for (_, v, m), sc in zip(blocks, srow):
            a, carry = _sb_weights(sc, m, carry)
            acc = acc + _dot(a, v)
        out.append((acc, carry))
    return out, extra


def _sb_prompt_kernel(bias_ref, q_ref, k_ref, v_ref, km_ref, vm_ref, u2_ref, o_ref, *, tq, hp):
    hb = pl.program_id(1)
    i = pl.program_id(2)
    u2 = u2_ref[...]
    cols = [slice(j * SB_HD, (j + 1) * SB_HD) for j in range(hp)]
    zb = [bias_ref[hb * hp + j] * LOG2E for j in range(hp)]
    q = [q_ref[:, c].astype(BF16) for c in cols]

    def kv(chunk, c):
        rows = pl.ds(pl.multiple_of(chunk * tq, tq), tq)
        return k_ref[rows, c].astype(BF16), v_ref[rows, c].astype(BF16)

    tri = lax.broadcasted_iota(jnp.int32, (tq, tq), 1) < lax.broadcasted_iota(jnp.int32, (tq, tq), 0)
    zero = (jnp.zeros((tq, SB_HD), F32), jnp.zeros((tq, 1), F32))
    mmask = lax.broadcasted_iota(jnp.int32, (tq, LANE), 1) < N_META
    state, meta = _sb_streams(
        [(q[j], zb[j], [kv(i, cols[j]) + (tri,)], *zero) for j in range(hp)], u2,
        scores_only=[(q[j], zb[j], km_ref[:, cols[j]].astype(BF16), mmask) for j in range(hp)])

    def pair(p, state):
        c0 = i - 1 - 2 * p
        return tuple(_sb_streams(
            [(q[j], zb[j], [kv(c0, cols[j]) + (None,), kv(c0 - 1, cols[j]) + (None,)], *state[j])
             for j in range(hp)], u2)[0])

    def single(_, state):
        return tuple(_sb_streams(
            [(q[j], zb[j], [kv(0, cols[j]) + (None,)], *state[j]) for j in range(hp)], u2)[0])

    state = lax.fori_loop(0, lax.shift_right_logical(i, 1), pair, tuple(state))
    state = lax.fori_loop(0, lax.bitwise_and(i, 1), single, state)

    for j in range(hp):
        acc, carry = state[j]
        a_m, _ = _sb_weights(meta[j], mmask, carry)
        o_ref[:, cols[j]] = acc + _dot(a_m, vm_ref[:, cols[j]].astype(BF16))


def _sb_prompt(u_p, u_m, sb_bias, u2, *, n_batch, seq, tq, hp):
    nq = seq // tq
    w = hp * SB_HD
    qb = COL_SQ // w
    kb = COL_SK // w
    vb = COL_SV // w
    kern = functools.partial(_sb_prompt_kernel, tq=tq, hp=hp)
    return pl.pallas_call(
        kern,
        grid=(n_batch, SB_HEADS // hp, nq),
        in_specs=[
            pl.BlockSpec(memory_space=pltpu.SMEM),
            pl.BlockSpec((tq, w), lambda b, h, i: (b * nq + i, qb + h)),
            pl.BlockSpec((seq, w), lambda b, h, i: (b, kb + h)),
            pl.BlockSpec((seq, w), lambda b, h, i: (b, vb + h)),
            pl.BlockSpec((LANE, w), lambda b, h, i: (0, kb + h)),
            pl.BlockSpec((LANE, w), lambda b, h, i: (0, vb + h)),
            pl.BlockSpec((2 * LANE, LANE), lambda b, h, i: (0, 0)),
        ],
        out_specs=pl.BlockSpec((tq, w), lambda b, h, i: (b * nq + i, h)),
        out_shape=jax.ShapeDtypeStruct((n_batch * seq, SB_WIDTH), F32),
        compiler_params=pltpu.CompilerParams(
            dimension_semantics=("arbitrary", "arbitrary", "arbitrary"), vmem_limit_bytes=VMEM_LIMIT),
        name="sb_prompt",
    )(sb_bias, u_p, u_p, u_p, u_m, u_m, u2)


def _gla_steps(chains, t_last):
    t_len = chains[0][1].shape[0]
    lane = lax.broadcasted_iota(jnp.int32, (t_len, t_len), 1)
    row = lax.broadcasted_iota(jnp.int32, (t_len, t_len), 0)
    prep = []
    for st_prev, q, k, v, b in chains:
        b_last = b[t_last:t_last + 1, :]
        lhs = jnp.concatenate(
            [(q * jnp.exp(jnp.minimum(b - b[s:s + 1, :], 0.0))).astype(BF16) for s in range(t_len)],
            axis=0)
        qe = (q * jnp.exp(b)).astype(BF16)
        kd = (k * jnp.exp(b_last - b)).astype(BF16)
        prep.append((lhs, qe, kd, k.astype(BF16), v.astype(BF16), jnp.exp(b_last)))
    mm = [(_dot_nt(lhs, k_bf), _dot_nt(qe, st_prev.astype(BF16)), _dot_tn(v_bf, kd))
          for (lhs, qe, kd, k_bf, v_bf, _), (st_prev, *_) in zip(prep, chains)]
    attns = []
    for r, _, _ in mm:
        attn = jnp.zeros((t_len, t_len), F32)
        for s in range(t_len):
            attn = jnp.where(lane == s, r[s * t_len:(s + 1) * t_len, :], attn)
        attns.append(jnp.where(lane <= row, attn, 0.0).astype(BF16))
    out = []
    for attn, (_, o_state, inc), (_, _, _, _, v_bf, dec), (st_prev, *_) in zip(attns, mm, prep, chains):
        out.append((st_prev * dec + inc, _dot(attn, v_bf) + o_state))
    return out


def _head_cols(h):
    return slice(h * GLA_DK, (h + 1) * GLA_DK), slice(h * GLA_DV, (h + 1) * GLA_DV)


def _gla_prompt_kernel(q_ref, k_ref, v_ref, b_ref, qm_ref, km_ref, vm_ref, bm_ref,
                       o_ref, s_ref, st_scr, *, n_steps, bb):
    blk = pl.program_id(1)
    scale = GLA_DK ** -0.5

    @pl.when(blk == 0)
    def _():
        chains = []
        for h in range(GLA_HEADS):
            kc, vc = _head_cols(h)
            chains.append((jnp.zeros((GLA_DV, GLA_DK), F32), qm_ref[:, kc] * scale, km_ref[:, kc],
                           vm_ref[:, vc], bm_ref[:, kc]))
        for h, (st1, _) in enumerate(_gla_steps(chains, GLA_STEP - 1)):
            for n in range(bb):
                st_scr[n, h] = st1

    def body(step, _):
        rows = pl.ds(pl.multiple_of(step * GLA_STEP, GLA_STEP), GLA_STEP)
        chains = []
        for n in range(bb):
            for h in range(GLA_HEADS):
                kc, vc = _head_cols(h)
                chains.append((st_scr[n, h], q_ref[n, rows, kc] * scale, k_ref[n, rows, kc],
                               v_ref[n, rows, vc], b_ref[n, rows, kc]))
        res = _gla_steps(chains, GLA_STEP - 1)
        for n in range(bb):
            for h in range(GLA_HEADS):
                st_scr[n, h] = res[n * GLA_HEADS + h][0]
            o_ref[n, rows, :] = jnp.concatenate(
                [res[n * GLA_HEADS + h][1] for h in range(GLA_HEADS)], axis=1)
        return 0

    lax.fori_loop(0, n_steps, body, 0)

    @pl.when(blk == pl.num_programs(1) - 1)
    def _():
        for n in range(bb):
            for h in range(GLA_HEADS):
                s_ref[n, h] = st_scr[n, h].T


def _gla_prompt(u_p, b_p, u_m, b_m, *, n_batch, seq, tb, bb):
    qb = COL_GQ // GLA_KEY
    kb = COL_GK // GLA_KEY
    vb = COL_GV // GLA_WIDTH
    assert n_batch % bb == 0 and seq % tb == 0
    kern = functools.partial(_gla_prompt_kernel, n_steps=tb // GLA_STEP, bb=bb)
    return pl.pallas_call(
        kern,
        grid=(n_batch // bb, seq // tb),
        in_specs=[
            pl.BlockSpec((bb, tb, GLA_KEY), lambda n, j: (n, j, qb)),
            pl.BlockSpec((bb, tb, GLA_KEY), lambda n, j: (n, j, kb)),
            pl.BlockSpec((bb, tb, GLA_WIDTH), lambda n, j: (n, j, vb)),
            pl.BlockSpec((bb, tb, GLA_KEY), lambda n, j: (n, j, 0)),
            pl.BlockSpec((N_META, GLA_KEY), lambda n, j: (0, qb)),
            pl.BlockSpec((N_META, GLA_KEY), lambda n, j: (0, kb)),
            pl.BlockSpec((N_META, GLA_WIDTH), lambda n, j: (0, vb)),
            pl.BlockSpec((N_META, GLA_KEY), lambda n, j: (0, 0)),
        ],
        out_specs=[
            pl.BlockSpec((bb, tb, GLA_WIDTH), lambda n, j: (n, j, 0)),
            pl.BlockSpec((bb, GLA_HEADS, GLA_DK, GLA_DV), lambda n, j: (n, 0, 0, 0)),
        ],
        out_shape=[
            jax.ShapeDtypeStruct((n_batch, seq, GLA_WIDTH), F32),
            jax.ShapeDtypeStruct((n_batch, GLA_HEADS, GLA_DK, GLA_DV), F32),
        ],
        scratch_shapes=[pltpu.VMEM((bb, GLA_HEADS, GLA_DV, GLA_DK), F32)],
        compiler_params=pltpu.CompilerParams(
            dimension_semantics=("arbitrary", "arbitrary"), vmem_limit_bytes=VMEM_LIMIT),
        name="gla_prompt",
    )(u_p, u_p, u_p, b_p, u_m, u_m, u_m, b_m)


def _gla_sample_kernel(q_ref, k_ref, v_ref, b_ref, s_in_ref, o_ref, s_out_ref, *, bb, dec_seq):
    scale = GLA_DK ** -0.5
    chains = []
    for n in range(bb):
        for h in range(GLA_HEADS):
            kc, vc = _head_cols(h)
            chains.append((s_in_ref[n, h].T, q_ref[n, :, kc] * scale, k_ref[n, :, kc],
                           v_ref[n, :, vc], b_ref[n, :, kc]))
    res = _gla_steps(chains, dec_seq - 1)
    for n in range(bb):
        for h in range(GLA_HEADS):
            s_out_ref[n, h] = res[n * GLA_HEADS + h][0].T
        o_ref[n] = jnp.concatenate([res[n * GLA_HEADS + h][1] for h in range(GLA_HEADS)], axis=1)


def _gla_sample(gq, gk, gv, gb, state, *, bb, dec_seq):
    db = gq.shape[0]
    assert db % bb == 0
    kern = functools.partial(_gla_sample_kernel, bb=bb, dec_seq=dec_seq)
    return pl.pallas_call(
        kern,
        grid=(db // bb,),
        in_specs=[
            pl.BlockSpec((bb, GLA_STEP, GLA_KEY), lambda n: (n, 0, 0)),
            pl.BlockSpec((bb, GLA_STEP, GLA_KEY), lambda n: (n, 0, 0)),
            pl.BlockSpec((bb, GLA_STEP, GLA_WIDTH), lambda n: (n, 0, 0)),
            pl.BlockSpec((bb, GLA_STEP, GLA_KEY), lambda n: (n, 0, 0)),
            pl.BlockSpec((bb, GLA_HEADS, GLA_DK, GLA_DV), lambda n: (n, 0, 0, 0)),
        ],
        out_specs=[
            pl.BlockSpec((bb, GLA_STEP, GLA_WIDTH), lambda n: (n, 0, 0)),
            pl.BlockSpec((bb, GLA_HEADS, GLA_DK, GLA_DV), lambda n: (n, 0, 0, 0)),
        ],
        out_shape=[
            jax.ShapeDtypeStruct((db, GLA_STEP, GLA_WIDTH), F32),
            jax.ShapeDtypeStruct((db, GLA_HEADS, GLA_DK, GLA_DV), F32),
        ],
        compiler_params=pltpu.CompilerParams(
            dimension_semantics=("arbitrary",), vmem_limit_bytes=VMEM_LIMIT),
        name="gla_sample",
    )(gq, gk, gv, gb, state)


PAGES_PER_STEP = 8


def _head_mask():
    head_of_col = lax.broadcasted_iota(jnp.int32, (SB_HEADS, SB_WIDTH), 1) // SB_HD
    return head_of_col == lax.broadcasted_iota(jnp.int32, (SB_HEADS, SB_WIDTH), 0)


def _sb_sample_kernel(pt_ref, q_ref, kn_ref, vn_ref, *refs, dec_seq):
    del pt_ref
    npg = PAGES_PER_STEP
    k_refs = refs[:npg]
    v_refs = refs[npg:2 * npg]
    u2_ref, bias_ref, o_ref, qbd_scr, k2d_scr, v2d_scr, acc_scr, c_scr = refs[2 * npg:]
    g = pl.program_id(1)
    n_rows = dec_seq * SB_HEADS
    u2 = u2_ref[...]
    zbias2 = bias_ref[...][:, :1] * LOG2E

    @pl.when(g == 0)
    def _():
        hmask = _head_mask()
        qbd = jnp.concatenate(
            [jnp.where(hmask, jnp.broadcast_to(q_ref[0, t:t + 1, :], (SB_HEADS, SB_WIDTH)), 0.0)
             for t in range(dec_seq)], axis=0)
        qbd_scr[...] = qbd.astype(BF16)
        lane = lax.broadcasted_iota(jnp.int32, (n_rows, LANE), 1)
        t_of_row = lax.broadcasted_iota(jnp.int32, (n_rows, LANE), 0) // SB_HEADS
        z_new = jnp.zeros((n_rows, LANE), F32)
        for t in range(dec_seq):
            zc = jnp.sum(qbd * kn_ref[0, t:t + 1, :], axis=-1, keepdims=True)
            z_new = jnp.where(lane == t, zc, z_new)
        mask = lane < t_of_row
        a, carry = _sb_weights(_sb_scores(z_new, zbias2, mask, u2), mask, jnp.zeros((n_rows, 1), F32))
        a = a.astype(F32)
        acc = jnp.zeros((n_rows, SB_WIDTH), F32)
        for t in range(dec_seq):
            acc = acc + a[:, t:t + 1] * vn_ref[0, t:t + 1, :]
        acc_scr[...] = acc
        c_scr[...] = jnp.broadcast_to(carry, c_scr.shape)

    for s in range(npg):
        for h in range(SB_HEADS):
            rows = slice(s * PAGE_SIZE, (s + 1) * PAGE_SIZE)
            cols = slice(h * SB_HD, (h + 1) * SB_HD)
            k2d_scr[rows, cols] = k_refs[s][pl.ds(h, PAGE_SIZE, stride=SB_HEADS), :].astype(BF16)
            v2d_scr[rows, cols] = v_refs[s][pl.ds(h, PAGE_SIZE, stride=SB_HEADS), :].astype(BF16)

    scores = _sb_scores(_dot_nt(qbd_scr[...], k2d_scr[...]), zbias2, None, u2)
    a, carry = _sb_weights(scores, None, c_scr[...][:, :1])
    acc_scr[...] += _dot(a, v2d_scr[...])
    c_scr[...] = jnp.broadcast_to(carry, c_scr.shape)

    @pl.when(g == pl.num_programs(1) - 1)
    def _():
        hmask = _head_mask()
        for t in range(dec_seq):
            blk = acc_scr[t * SB_HEADS:(t + 1) * SB_HEADS, :]
            o_ref[0, t:t + 1, :] = jnp.sum(jnp.where(hmask, blk, 0.0), axis=0, keepdims=True)


def _sb_sample(page_table, q, k_new, v_new, cache_k, cache_v, u2, bias_rows):
    db, dec_seq, _ = q.shape
    n_pages = page_table.shape[1]
    npg = PAGES_PER_STEP
    assert n_pages % npg == 0
    n_groups = n_pages // npg
    n_rows = dec_seq * SB_HEADS
    page_rows = PAGE_SIZE * SB_HEADS
    pt_flat = page_table.reshape(-1)

    def page_spec(s):
        return pl.BlockSpec(
            (page_rows, SB_HD),
            lambda b, g, pt: (pt[b * n_pages + (n_groups - 1 - g) * npg + s], 0))

    small = pl.BlockSpec((1, dec_seq, SB_WIDTH), lambda b, g, pt: (b, 0, 0))
    grid_spec = pltpu.PrefetchScalarGridSpec(
        num_scalar_prefetch=1,
        grid=(db, n_groups),
        in_specs=[small, small, small]
        + [page_spec(s) for s in range(npg)]
        + [page_spec(s) for s in range(npg)]
        + [pl.BlockSpec((2 * LANE, LANE), lambda b, g, pt: (0, 0)),
           pl.BlockSpec((n_rows, LANE), lambda b, g, pt: (0, 0))],
        out_specs=pl.BlockSpec((1, dec_seq, SB_WIDTH), lambda b, g, pt: (b, 0, 0)),
        scratch_shapes=[
            pltpu.VMEM((n_rows, SB_WIDTH), BF16),
            pltpu.VMEM((npg * PAGE_SIZE, SB_WIDTH), BF16),
            pltpu.VMEM((npg * PAGE_SIZE, SB_WIDTH), BF16),
            pltpu.VMEM((n_rows, SB_WIDTH), F32),
            pltpu.VMEM((n_rows, LANE), F32),
        ],
    )
    kern = functools.partial(_sb_sample_kernel, dec_seq=dec_seq)
    return pl.pallas_call(
        kern,
        grid_spec=grid_spec,
        out_shape=jax.ShapeDtypeStruct((db, dec_seq, SB_WIDTH), F32),
        compiler_params=pltpu.CompilerParams(
            dimension_semantics=("arbitrary", "arbitrary"), vmem_limit_bytes=VMEM_LIMIT),
        name="sb_sample",
    )(pt_flat, q, k_new, v_new, *([cache_k] * npg), *([cache_v] * npg), u2, bias_rows)


def _out_proj_kernel(osb_ref, sg_ref, ogla_ref, gg_ref, x_ref, gng_ref, w_ref, png_ref, y_ref):
    m1 = osb_ref[...] * jax.nn.silu(sg_ref[...])
    og = ogla_ref[...]
    gng = gng_ref[...]
    normed = []
    for h in range(GLA_HEADS):
        blk = og[:, h * GLA_DV:(h + 1) * GLA_DV]
        var = jnp.mean(blk * blk, axis=-1, keepdims=True)
        normed.append(blk * lax.rsqrt(var + EPS) * gng)
    m2 = jnp.concatenate(normed, axis=1) * jax.nn.silu(gg_ref[...])
    mixed = jnp.concatenate([m1, m2], axis=1).astype(BF16)
    br = _dot(mixed, w_ref[...])
    var = jnp.mean(br * br, axis=-1, keepdims=True)
    y_ref[...] = x_ref[...] + br * lax.rsqrt(var + EPS) * png_ref[...]


def _out_proj(o_sb, u, o_gla, x, gng, w_out, png, *, tm):
    m, d = x.shape
    assert m % tm == 0
    sgb = COL_SG // SB_WIDTH
    ggb = COL_GG // GLA_WIDTH
    mix = SB_WIDTH + GLA_WIDTH
    return pl.pallas_call(
        _out_proj_kernel,
        grid=(m // tm,),
        in_specs=[
            pl.BlockSpec((tm, SB_WIDTH), lambda i: (i, 0)),
            pl.BlockSpec((tm, SB_WIDTH), lambda i: (i, sgb)),
            pl.BlockSpec((tm, GLA_WIDTH), lambda i: (i, 0)),
            pl.BlockSpec((tm, GLA_WIDTH), lambda i: (i, ggb)),
            pl.BlockSpec((tm, d), lambda i: (i, 0)),
            pl.BlockSpec((1, GLA_DV), lambda i: (0, 0)),
            pl.BlockSpec((mix, d), lambda i: (0, 0)),
            pl.BlockSpec((1, d), lambda i: (0, 0)),
        ],
        out_specs=pl.BlockSpec((tm, d), lambda i: (i, 0)),
        out_shape=jax.ShapeDtypeStruct((m, d), F32),
        compiler_params=pltpu.CompilerParams(
            dimension_semantics=("arbitrary",), vmem_limit_bytes=VMEM_LIMIT),
        name="out_proj",
    )(o_sb, u, o_gla, u, x, gng, w_out, png)


def _tri2(n):
    r = jnp.arange(n)[:, None]
    c = jnp.arange(n)[None, :]
    u = (c < r).astype(BF16)
    return jnp.concatenate([u, u], axis=0)


def _seg_tri(block):
    r = jnp.arange(LANE)[:, None]
    c = jnp.arange(LANE)[None, :]
    return ((c <= r) & (c // block == r // block)).astype(BF16)


def _largest_tile(m, cap, quantum=SUBLANE):
    for t in range(min(cap, m) // quantum * quantum, 0, -quantum):
        if m % t == 0:
            return t
    raise ValueError(f"no {quantum}-aligned tile divides {m}")


def kernel(x_prompt, x_sample, cache_k, cache_v, state_gla, page_table, meta_tokens,
           norm_pre_g, w_in, sb_bias, w_alpha, b_alpha, gla_norm_g, w_out, norm_post_g):
    n_batch, seq, d = x_prompt.shape
    db, dec_seq, _ = x_sample.shape
    assert w_in.shape[0] == 1 and GLA_STEP % dec_seq == 0 and seq % GLA_STEP == 0
    n_prompt = n_batch * seq
    n_sample = db * dec_seq
    xp = x_prompt.reshape(n_prompt, d)
    xs = x_sample.reshape(n_sample, d)
    xm = jnp.pad(meta_tokens, ((0, LANE - N_META), (0, 0)))

    w_main = w_in[0, :, :N_MAIN].astype(BF16)
    w_ga = jnp.pad(w_in[0, :, N_MAIN:], ((0, 0), (0, LANE - GLA_RANK))).astype(BF16)
    w_al = jnp.pad(w_alpha[0], ((0, LANE - GLA_RANK), (0, 0))).astype(BF16)
    proj = functools.partial(_in_proj, g_pre=norm_pre_g, w_main=w_main, w_ga=w_ga, w_al=w_al,
                             b_al=b_alpha, tn=1024)
    step_seg = _seg_tri(GLA_STEP)
    u_p, b_p = proj(xp, seg=step_seg, tm=_largest_tile(n_prompt, 1024, LANE))
    u_s, b_s = proj(xs, seg=_seg_tri(dec_seq), tm=_largest_tile(n_sample, 1024, LANE))
    u_m, b_m = proj(xm, seg=step_seg, tm=LANE)

    w_out_bf = w_out[0].astype(BF16)

    tq = 256
    u2 = _tri2(LANE)
    o_sb_p = _sb_prompt(u_p, u_m, sb_bias[0], u2, n_batch=n_batch, seq=seq, tq=tq, hp=4)
    o_gla_p, s_end = _gla_prompt(u_p.reshape(n_batch, seq, N_MAIN), b_p.reshape(n_batch, seq, GLA_KEY),
                                 u_m, b_m, n_batch=n_batch, seq=seq, tb=_largest_tile(seq, 256),
                                 bb=_largest_tile(n_batch, 4, 1))
    y_prompt = _out_proj(o_sb_p, u_p, o_gla_p.reshape(n_prompt, GLA_WIDTH), xp, gla_norm_g, w_out_bf,
                         norm_post_g, tm=_largest_tile(n_prompt, 256)).reshape(n_batch, seq, d)

    q_s = u_s[:, COL_SQ:COL_SQ + SB_WIDTH].reshape(db, dec_seq, SB_WIDTH)
    k_s = u_s[:, COL_SK:COL_SK + SB_WIDTH].reshape(db, dec_seq, SB_WIDTH)
    v_s = u_s[:, COL_SV:COL_SV + SB_WIDTH].reshape(db, dec_seq, SB_WIDTH)
    n_pool = cache_k.shape[1]
    ck = cache_k.reshape(n_pool * PAGE_SIZE * SB_HEADS, SB_HD)
    cv = cache_v.reshape(n_pool * PAGE_SIZE * SB_HEADS, SB_HD)
    bias_rows = jnp.broadcast_to(jnp.tile(sb_bias[0], dec_seq)[:, None], (dec_seq * SB_HEADS, LANE))
    o_sb_s = _sb_sample(page_table, q_s, k_s, v_s, ck, cv, u2, bias_rows)

    pad_rows = lambda t: jnp.pad(t.reshape(db, dec_seq, -1), ((0, 0), (0, GLA_STEP - dec_seq), (0, 0)))
    o_gla_s, s_new = _gla_sample(
        pad_rows(u_s[:, COL_GQ:COL_GQ + GLA_KEY]), pad_rows(u_s[:, COL_GK:COL_GK + GLA_KEY]),
        pad_rows(u_s[:, COL_GV:COL_GV + GLA_WIDTH]), pad_rows(b_s), state_gla[0],
        bb=_largest_tile(db, 4, 1), dec_seq=dec_seq)
    y_sample = _out_proj(o_sb_s.reshape(n_sample, SB_WIDTH), u_s,
                         o_gla_s[:, :dec_seq].reshape(n_sample, GLA_WIDTH), xs, gla_norm_g, w_out_bf,
                         norm_post_g, tm=_largest_tile(n_sample, 256)).reshape(db, dec_seq, d)

    def with_meta(col):
        real = u_p[:, col:col + SB_WIDTH].reshape(n_batch, seq, SB_HEADS, SB_HD)
        meta = jnp.broadcast_to(u_m[:N_META, col:col + SB_WIDTH].reshape(1, N_META, SB_HEADS, SB_HD),
                                (n_batch, N_META, SB_HEADS, SB_HD))
        return jnp.concatenate([meta, real], axis=1)[None]
    new_k_prompt = with_meta(COL_SK)
    new_v_prompt = with_meta(COL_SV)
    new_k_sample = k_s.reshape(1, db, dec_seq, SB_HEADS, SB_HD)
    new_v_sample = v_s.reshape(1, db, dec_seq, SB_HEADS, SB_HD)
    return (y_prompt, y_sample, new_k_prompt, new_v_prompt, s_end[None],
            new_k_sample, new_v_sample, s_new[None])
```

```python
import functools
import math

import jax
import jax.numpy as jnp
from jax import lax
from jax.experimental import pallas as pl
from jax.experimental.pallas import tpu as pltpu

F32 = jnp.float32
BF16 = jnp.bfloat16

N_META = 16
SB_HEADS = 8
SB_HD = 128
GLA_HEADS = 4
GLA_DK = 128
GLA_DV = 256
GLA_RANK = 16
GLA_TAU = 16.0
PAGE_SIZE = 128
EPS = 1e-6

LANE = 128
SUBLANE = 8
VMEM_LIMIT = 56 * 1024 * 1024
LOG2E = math.log2(math.e)

SB_WIDTH = SB_HEADS * SB_HD
GLA_KEY = GLA_HEADS * GLA_DK
GLA_WIDTH = GLA_HEADS * GLA_DV
N_MAIN = 4 * SB_WIDTH + 2 * GLA_KEY + 2 * GLA_WIDTH
COL_SQ, COL_SK, COL_SV, COL_SG = 0, SB_WIDTH, 2 * SB_WIDTH, 3 * SB_WIDTH
COL_GQ = 4 * SB_WIDTH
COL_GK = COL_GQ + GLA_KEY
COL_GV = COL_GK + GLA_KEY
COL_GG = COL_GV + GLA_WIDTH

GLA_STEP = 2 * SUBLANE
assert N_META == GLA_STEP


def _log_sigmoid(z):
    return jnp.minimum(z, 0.0) - jnp.log1p(jnp.exp(-jnp.abs(z)))


def _split_bf16(x):
    hi = x.astype(BF16)
    lo = (x - hi.astype(F32)).astype(BF16)
    return hi, lo


def _dot(a, b):
    return jnp.dot(a, b, preferred_element_type=F32)


def _dot_nt(a, b):
    return lax.dot_general(a, b, (((1,), (1,)), ((), ())), preferred_element_type=F32)


def _dot_tn(a, b):
    return lax.dot_general(a, b, (((0,), (0,)), ((), ())), preferred_element_type=F32)


def _in_proj_kernel(x_ref, g_ref, w_ref, wga_ref, wal_ref, bal_ref, seg_ref, u_ref, lf_ref, *rest,
                    n_first, cache_cols):
    h_scr = rest[-1]
    j = pl.program_id(1)

    @pl.when(j == 0)
    def _():
        x = x_ref[...]
        var = jnp.mean(x * x, axis=-1, keepdims=True)
        h = (x * lax.rsqrt(var + EPS) * g_ref[...]).astype(BF16)
        h_scr[...] = h
        ga = _dot(h, wga_ref[...])
        pre = _dot(ga.astype(BF16), wal_ref[...]) + bal_ref[...]
        g_hi, g_lo = _split_bf16(_log_sigmoid(pre) / GLA_TAU)
        for r in range(x.shape[0] // LANE):
            rows = slice(r * LANE, (r + 1) * LANE)
            seg = seg_ref[0 if r < n_first else 1]
            lf_ref[rows, :] = _dot(seg, g_hi[rows]) + _dot(seg, g_lo[rows])

    res = _dot(h_scr[...], w_ref[...])
    u_ref[...] = res
    for ref, col in zip(rest[:-1], cache_cols):
        @pl.when(j == col)
        def _(ref=ref):
            for h in range(SB_HEADS):
                ref[pl.ds(h, res.shape[0], stride=SB_HEADS), :] = res[:, h * SB_HD:(h + 1) * SB_HD]


def _in_proj(x, g_pre, w_main, w_ga, w_al, b_al, seg, *, tm, tn, n_first=None, cache_rows=None):
    m, d = x.shape
    n = w_main.shape[1]
    assert m % tm == 0 and n % tn == 0 and tm % LANE == 0
    n_first = tm // LANE if n_first is None else n_first
    out_specs = [
        pl.BlockSpec((tm, tn), lambda i, j: (i, j)),
        pl.BlockSpec((tm, GLA_KEY), lambda i, j: (i, 0)),
    ]
    out_shape = [
        jax.ShapeDtypeStruct((m, n), F32),
        jax.ShapeDtypeStruct((m, GLA_KEY), F32),
    ]
    cache_cols = ()
    if cache_rows is not None:
        seq, lead = cache_rows
        assert tn == SB_WIDTH and seq % tm == 0 and m % seq == 0
        per_seq = seq // tm
        cache_cols = (COL_SK // tn, COL_SV // tn)

        def cache_index(i, j):
            return (((i // per_seq) * (seq + lead) + lead + (i % per_seq) * tm) * SB_HEADS, 0)

        flat = jax.ShapeDtypeStruct(((m // seq) * (seq + lead) * SB_HEADS, SB_HD), F32)
        for _ in cache_cols:
            out_specs.append(pl.BlockSpec((pl.Element(tm * SB_HEADS), pl.Element(SB_HD)), cache_index))
            out_shape.append(flat)
    kern = functools.partial(_in_proj_kernel, n_first=n_first, cache_cols=cache_cols)
    return pl.pallas_call(
        kern,
        grid=(m // tm, n // tn),
        in_specs=[
            pl.BlockSpec((tm, d), lambda i, j: (i, 0)),
            pl.BlockSpec((1, d), lambda i, j: (0, 0)),
            pl.BlockSpec((d, tn), lambda i, j: (0, j)),
            pl.BlockSpec((d, LANE), lambda i, j: (0, 0)),
            pl.BlockSpec((LANE, GLA_KEY), lambda i, j: (0, 0)),
            pl.BlockSpec((1, GLA_KEY), lambda i, j: (0, 0)),
            pl.BlockSpec((2, LANE, LANE), lambda i, j: (0, 0, 0)),
        ],
        out_specs=out_specs,
        out_shape=out_shape,
        scratch_shapes=[pltpu.VMEM((tm, d), BF16)],
        compiler_params=pltpu.CompilerParams(
            dimension_semantics=("arbitrary", "arbitrary"), vmem_limit_bytes=VMEM_LIMIT),
        name="in_proj",
    )(x, g_pre, w_main, w_ga, w_al, b_al, seg)


def _meta_fill_kernel(k_any, v_any, mk_ref, mv_ref, ko_ref, vo_ref):
    del k_any, v_any
    for ref, src in ((ko_ref, mk_ref), (vo_ref, mv_ref)):
        for h in range(SB_HEADS):
            ref[pl.ds(h, N_META, stride=SB_HEADS), :] = src[:N_META, h * SB_HD:(h + 1) * SB_HD]


def _meta_fill(k_flat, v_flat, u_m, *, n_seq, meta_row):
    rows = N_META * SB_HEADS
    per_seq = k_flat.shape[0] // n_seq
    assert per_seq % rows == 0
    blocks = per_seq // rows
    out = pl.BlockSpec((rows, SB_HD), lambda b: (b * blocks, 0))
    return pl.pallas_call(
        _meta_fill_kernel,
        grid=(n_seq,),
        in_specs=[
            pl.BlockSpec(memory_space=pl.ANY),
            pl.BlockSpec(memory_space=pl.ANY),
            pl.BlockSpec((LANE, SB_WIDTH), lambda b: (meta_row // LANE, COL_SK // SB_WIDTH)),
            pl.BlockSpec((LANE, SB_WIDTH), lambda b: (meta_row // LANE, COL_SV // SB_WIDTH)),
        ],
        out_specs=[out, out],
        out_shape=[jax.ShapeDtypeStruct(k_flat.shape, F32), jax.ShapeDtypeStruct(v_flat.shape, F32)],
        input_output_aliases={0: 0, 1: 1},
        compiler_params=pltpu.CompilerParams(dimension_semantics=("arbitrary",)),
        name="meta_fill",
    )(k_flat, v_flat, u_m, u_m)


SB_ZSCALE2 = (SB_HD ** -0.5) * LOG2E


def _softplus2(z_raw, zbias2):
    z2 = z_raw * SB_ZSCALE2 + zbias2
    t2 = jnp.log(1.0 + jnp.exp2(-jnp.abs(z2))) * LOG2E
    sp2 = jnp.maximum(z2, 0.0) + t2
    return sp2, z2 - sp2


def _tri_cumsum(sp2, u2):
    hi, lo = _split_bf16(sp2)
    return _dot(jnp.concatenate([hi, lo], axis=1), u2)


def _sb_scores(z_raw, zbias2, mask, u2):
    sp2, ls2 = _softplus2(z_raw, zbias2)
    if mask is not None:
        sp2 = jnp.where(mask, sp2, 0.0)
    subs = [sp2[:, j * LANE:(j + 1) * LANE] for j in range(sp2.shape[1] // LANE)]
    loc = _tri_cumsum(jnp.concatenate(subs, axis=0), u2)
    return ls2, loc, [jnp.sum(s, axis=-1, keepdims=True) for s in subs]


def _sb_weights(scores, mask, carry):
    ls2, loc, sums = scores
    rows = ls2.shape[0]
    later = [None] * len(sums)
    for j in range(len(sums) - 1, -1, -1):
        later[j] = loc[j * rows:(j + 1) * rows] + carry
        carry = carry + sums[j]
    a = jnp.exp2(ls2 - jnp.concatenate(later, axis=1))
    if mask is not None:
        a = jnp.where(mask, a, 0.0)
    return a.astype(BF16), carry


def _sb_streams(streams, u2, scores_only=()):
    zs = [[_dot_nt(q, k) for k, _, _ in blocks] for q, _, blocks, _, _ in streams]
    zx = [_dot_nt(q, k) for q, _, k, _ in scores_only]
    scores = [[_sb_scores(z, zb, m, u2) for z, (_, _, m) in zip(zrow, blocks)]
              for zrow, (_, zb, blocks, _, _) in zip(zs, streams)]
    extra = [_sb_scores(z, zb, m, u2) for z, (_, zb, _, m) in zip(zx, scores_only)]
    out = []
    for (_, _, blocks, acc, carry), srow in zip(streams, scores):
        for (_, v, m), sc in zip(blocks, srow):
            a, carry = _sb_weights(sc, m, carry)
            acc = acc + _dot(a, v)
        out.append((acc, carry))
    return out, extra


def _sb_prompt_kernel(bias_ref, q_ref, k_ref, v_ref, km_ref, vm_ref, u2_ref, o_ref, *, tq, hp):
    hb = pl.program_id(1)
    i = pl.program_id(2)
    u2 = u2_ref[...]
    cols = [slice(j * SB_HD, (j + 1) * SB_HD) for j in range(hp)]
    zb = [bias_ref[hb * hp + j] * LOG2E for j in range(hp)]
    q = [q_ref[:, c].astype(BF16) for c in cols]

    def kv(chunk, c):
        rows = pl.ds(pl.multiple_of(chunk * tq, tq), tq)
        return k_ref[rows, c].astype(BF16), v_ref[rows, c].astype(BF16)

    tri = lax.broadcasted_iota(jnp.int32, (tq, tq), 1) < lax.broadcasted_iota(jnp.int32, (tq, tq), 0)
    zero = (jnp.zeros((tq, SB_HD), F32), jnp.zeros((tq, 1), F32))
    mmask = lax.broadcasted_iota(jnp.int32, (tq, LANE), 1) < N_META
    state, meta = _sb_streams(
        [(q[j], zb[j], [kv(i, cols[j]) + (tri,)], *zero) for j in range(hp)], u2,
        scores_only=[(q[j], zb[j], km_ref[:, cols[j]].astype(BF16), mmask) for j in range(hp)])

    def pair(p, state):
        c0 = i - 1 - 2 * p
        return tuple(_sb_streams(
            [(q[j], zb[j], [kv(c0, cols[j]) + (None,), kv(c0 - 1, cols[j]) + (None,)], *state[j])
             for j in range(hp)], u2)[0])

    def single(_, state):
        return tuple(_sb_streams(
            [(q[j], zb[j], [kv(0, cols[j]) + (None,)], *state[j]) for j in range(hp)], u2)[0])

    state = lax.fori_loop(0, lax.shift_right_logical(i, 1), pair, tuple(state))
    state = lax.fori_loop(0, lax.bitwise_and(i, 1), single, state)

    for j in range(hp):
        acc, carry = state[j]
        a_m, _ = _sb_weights(meta[j], mmask, carry)
        o_ref[:, cols[j]] = acc + _dot(a_m, vm_ref[:, cols[j]].astype(BF16))


def _sb_prompt(u_p, u_m, sb_bias, u2, *, n_batch, seq, tq, hp, meta_row):
    nq = seq // tq
    w = hp * SB_HD
    qb = COL_SQ // w
    kb = COL_SK // w
    vb = COL_SV // w
    kern = functools.partial(_sb_prompt_kernel, tq=tq, hp=hp)
    return pl.pallas_call(
        kern,
        grid=(n_batch, SB_HEADS // hp, nq),
        in_specs=[
            pl.BlockSpec(memory_space=pltpu.SMEM),
            pl.BlockSpec((tq, w), lambda b, h, i: (b * nq + i, qb + h)),
            pl.BlockSpec((seq, w), lambda b, h, i: (b, kb + h)),
            pl.BlockSpec((seq, w), lambda b, h, i: (b, vb + h)),
            pl.BlockSpec((LANE, w), lambda b, h, i: (meta_row // LANE, kb + h)),
            pl.BlockSpec((LANE, w), lambda b, h, i: (meta_row // LANE, vb + h)),
            pl.BlockSpec((2 * LANE, LANE), lambda b, h, i: (0, 0)),
        ],
        out_specs=pl.BlockSpec((tq, w), lambda b, h, i: (b * nq + i, h)),
        out_shape=jax.ShapeDtypeStruct((n_batch * seq, SB_WIDTH), F32),
        compiler_params=pltpu.CompilerParams(
            dimension_semantics=("arbitrary", "arbitrary", "arbitrary"), vmem_limit_bytes=VMEM_LIMIT),
        name="sb_prompt",
    )(sb_bias, u_p, u_p, u_p, u_m, u_m, u2)


def _gla_steps(chains, t_last):
    t_len = chains[0][1].shape[0]
    lane = lax.broadcasted_iota(jnp.int32, (t_len, t_len), 1)
    row = lax.broadcasted_iota(jnp.int32, (t_len, t_len), 0)
    prep = []
    for st_prev, q, k, v, b in chains:
        b_last = b[t_last:t_last + 1, :]
        lhs = jnp.concatenate(
            [(q * jnp.exp(jnp.minimum(b - b[s:s + 1, :], 0.0))).astype(BF16) for s in range(t_len)],
            axis=0)
        qe = (q * jnp.exp(b)).astype(BF16)
        kd = (k * jnp.exp(b_last - b)).astype(BF16)
        prep.append((lhs, qe, kd, k.astype(BF16), v.astype(BF16), jnp.exp(b_last)))
    mm = [(_dot_nt(lhs, k_bf), _dot_nt(qe, st_prev.astype(BF16)), _dot_tn(v_bf, kd))
          for (lhs, qe, kd, k_bf, v_bf, _), (st_prev, *_) in zip(prep, chains)]
    attns = []
    for r, _, _ in mm:
        attn = jnp.zeros((t_len, t_len), F32)
        for s in range(t_len):
            attn = jnp.where(lane == s, r[s * t_len:(s + 1) * t_len, :], attn)
        attns.append(jnp.where(lane <= row, attn, 0.0).astype(BF16))
    out = []
    for attn, (_, o_state, inc), (_, _, _, _, v_bf, dec), (st_prev, *_) in zip(attns, mm, prep, chains):
        out.append((st_prev * dec + inc, _dot(attn, v_bf) + o_state))
    return out


def _head_cols(h):
    return slice(h * GLA_DK, (h + 1) * GLA_DK), slice(h * GLA_DV, (h + 1) * GLA_DV)


def _gla_prompt_kernel(q_ref, k_ref, v_ref, b_ref, qm_ref, km_ref, vm_ref, bm_ref,
                       o_ref, s_ref, st_scr, *, n_steps, bb):
    blk = pl.program_id(1)
    scale = GLA_DK ** -0.5

    @pl.when(blk == 0)
    def _():
        chains = []
        for h in range(GLA_HEADS):
            kc, vc = _head_cols(h)
            chains.append((jnp.zeros((GLA_DV, GLA_DK), F32), qm_ref[:, kc] * scale, km_ref[:, kc],
                           vm_ref[:, vc], bm_ref[:, kc]))
        for h, (st1, _) in enumerate(_gla_steps(chains, GLA_STEP - 1)):
            for n in range(bb):
                st_scr[n, h] = st1

    def body(step, _):
        rows = pl.ds(pl.multiple_of(step * GLA_STEP, GLA_STEP), GLA_STEP)
        chains = []
        for n in range(bb):
            for h in range(GLA_HEADS):
                kc, vc = _head_cols(h)
                chains.append((st_scr[n, h], q_ref[n, rows, kc] * scale, k_ref[n, rows, kc],
                               v_ref[n, rows, vc], b_ref[n, rows, kc]))
        res = _gla_steps(chains, GLA_STEP - 1)
        for n in range(bb):
            for h in range(GLA_HEADS):
                st_scr[n, h] = res[n * GLA_HEADS + h][0]
            o_ref[n, rows, :] = jnp.concatenate(
                [res[n * GLA_HEADS + h][1] for h in range(GLA_HEADS)], axis=1)
        return 0

    lax.fori_loop(0, n_steps, body, 0)

    @pl.when(blk == pl.num_programs(1) - 1)
    def _():
        for n in range(bb):
            for h in range(GLA_HEADS):
                s_ref[n, h] = st_scr[n, h].T


def _gla_prompt(u_p, b_p, u_m, b_m, *, n_batch, seq, tb, bb, meta_row):
    qb = COL_GQ // GLA_KEY
    kb = COL_GK // GLA_KEY
    vb = COL_GV // GLA_WIDTH
    assert n_batch % bb == 0 and seq % tb == 0
    kern = functools.partial(_gla_prompt_kernel, n_steps=tb // GLA_STEP, bb=bb)
    return pl.pallas_call(
        kern,
        grid=(n_batch // bb, seq // tb),
        in_specs=[
            pl.BlockSpec((bb, tb, GLA_KEY), lambda n, j: (n, j, qb)),
            pl.BlockSpec((bb, tb, GLA_KEY), lambda n, j: (n, j, kb)),
            pl.BlockSpec((bb, tb, GLA_WIDTH), lambda n, j: (n, j, vb)),
            pl.BlockSpec((bb, tb, GLA_KEY), lambda n, j: (n, j, 0)),
            pl.BlockSpec((N_META, GLA_KEY), lambda n, j: (meta_row // N_META, qb)),
            pl.BlockSpec((N_META, GLA_KEY), lambda n, j: (meta_row // N_META, kb)),
            pl.BlockSpec((N_META, GLA_WIDTH), lambda n, j: (meta_row // N_META, vb)),
            pl.BlockSpec((N_META, GLA_KEY), lambda n, j: (meta_row // N_META, 0)),
        ],
        out_specs=[
            pl.BlockSpec((bb, tb, GLA_WIDTH), lambda n, j: (n, j, 0)),
            pl.BlockSpec((bb, GLA_HEADS, GLA_DK, GLA_DV), lambda n, j: (n, 0, 0, 0)),
        ],
        out_shape=[
            jax.ShapeDtypeStruct((n_batch, seq, GLA_WIDTH), F32),
            jax.ShapeDtypeStruct((n_batch, GLA_HEADS, GLA_DK, GLA_DV), F32),
        ],
        scratch_shapes=[pltpu.VMEM((bb, GLA_HEADS, GLA_DV, GLA_DK), F32)],
        compiler_params=pltpu.CompilerParams(
            dimension_semantics=("arbitrary", "arbitrary"), vmem_limit_bytes=VMEM_LIMIT),
        name="gla_prompt",
    )(u_p, u_p, u_p, b_p, u_m, u_m, u_m, b_m)


def _gla_sample_kernel(q_ref, k_ref, v_ref, b_ref, s_in_ref, o_ref, s_out_ref, *, bb, dec_seq):
    scale = GLA_DK ** -0.5
    chains = []
    for n in range(bb):
        for h in range(GLA_HEADS):
            kc, vc = _head_cols(h)
            chains.append((s_in_ref[n, h].T, q_ref[n, :, kc] * scale, k_ref[n, :, kc],
                           v_ref[n, :, vc], b_ref[n, :, kc]))
    res = _gla_steps(chains, dec_seq - 1)
    for n in range(bb):
        for h in range(GLA_HEADS):
            s_out_ref[n, h] = res[n * GLA_HEADS + h][0].T
        o_ref[n] = jnp.concatenate([res[n * GLA_HEADS + h][1] for h in range(GLA_HEADS)], axis=1)


def _gla_sample(gq, gk, gv, gb, state, *, bb, dec_seq):
    db = gq.shape[0]
    assert db % bb == 0
    kern = functools.partial(_gla_sample_kernel, bb=bb, dec_seq=dec_seq)
    return pl.pallas_call(
        kern,
        grid=(db // bb,),
        in_specs=[
            pl.BlockSpec((bb, GLA_STEP, GLA_KEY), lambda n: (n, 0, 0)),
            pl.BlockSpec((bb, GLA_STEP, GLA_KEY), lambda n: (n, 0, 0)),
            pl.BlockSpec((bb, GLA_STEP, GLA_WIDTH), lambda n: (n, 0, 0)),
            pl.BlockSpec((bb, GLA_STEP, GLA_KEY), lambda n: (n, 0, 0)),
            pl.BlockSpec((bb, GLA_HEADS, GLA_DK, GLA_DV), lambda n: (n, 0, 0, 0)),
        ],
        out_specs=[
            pl.BlockSpec((bb, GLA_STEP, GLA_WIDTH), lambda n: (n, 0, 0)),
            pl.BlockSpec((bb, GLA_HEADS, GLA_DK, GLA_DV), lambda n: (n, 0, 0, 0)),
        ],
        out_shape=[
            jax.ShapeDtypeStruct((db, GLA_STEP, GLA_WIDTH), F32),
            jax.ShapeDtypeStruct((db, GLA_HEADS, GLA_DK, GLA_DV), F32),
        ],
        compiler_params=pltpu.CompilerParams(
            dimension_semantics=("arbitrary",), vmem_limit_bytes=VMEM_LIMIT),
        name="gla_sample",
    )(gq, gk, gv, gb, state)


def _head_mask():
    head_of_col = lax.broadcasted_iota(jnp.int32, (SB_HEADS, SB_WIDTH), 1) // SB_HD
    return head_of_col == lax.broadcasted_iota(jnp.int32, (SB_HEADS, SB_WIDTH), 0)


def _sb_sample_kernel(pt_ref, q_ref, kn_ref, vn_ref, *refs, dec_seq, npg):
    del pt_ref
    k_refs = refs[:npg]
    v_refs = refs[npg:2 * npg]
    u2_ref, bias_ref, spread_ref, o_ref, k2d_scr = refs[2 * npg:]
    n_rows = dec_seq * SB_HEADS
    u2 = u2_ref[...]
    zbias2 = bias_ref[...][:, :1] * LOG2E

    hmask = _head_mask()
    qbd = jnp.concatenate(
        [jnp.where(hmask, jnp.broadcast_to(q_ref[0, t:t + 1, :], (SB_HEADS, SB_WIDTH)), 0.0)
         for t in range(dec_seq)], axis=0)

    lane = lax.broadcasted_iota(jnp.int32, (n_rows, LANE), 1)
    t_of_row = lax.broadcasted_iota(jnp.int32, (n_rows, LANE), 0) // SB_HEADS
    z_new = jnp.zeros((n_rows, LANE), F32)
    for t in range(dec_seq):
        zc = jnp.sum(qbd * kn_ref[0, t:t + 1, :], axis=-1, keepdims=True)
        z_new = jnp.where(lane == t, zc, z_new)
    mask = lane < t_of_row
    a_new, carry = _sb_weights(_sb_scores(z_new, zbias2, mask, u2), mask, jnp.zeros((n_rows, 1), F32))
    a_new = a_new.astype(F32)
    acc = jnp.zeros((n_rows, SB_HD), F32)
    for t in range(dec_seq):
        v_heads = jnp.concatenate(
            [vn_ref[0, t:t + 1, h * SB_HD:(h + 1) * SB_HD] for h in range(SB_HEADS)], axis=0)
        acc = acc + a_new[:, t:t + 1] * jnp.concatenate([v_heads] * dec_seq, axis=0)

    for s in range(npg):
        for h in range(SB_HEADS):
            rows = slice(s * PAGE_SIZE, (s + 1) * PAGE_SIZE)
            cols = slice(h * SB_HD, (h + 1) * SB_HD)
            k2d_scr[rows, cols] = k_refs[s][pl.ds(h, PAGE_SIZE, stride=SB_HEADS), :].astype(BF16)
    scores = _sb_scores(_dot_nt(qbd.astype(BF16), k2d_scr[...]), zbias2, None, u2)
    a, _ = _sb_weights(scores, None, carry)

    a_pages = jnp.concatenate([a[:, p * PAGE_SIZE:(p + 1) * PAGE_SIZE] for p in range(npg)], axis=0)
    a_spread = _dot(a_pages, spread_ref[...])
    flat = (n_rows, PAGE_SIZE * SB_HEADS)
    own_head = (lax.broadcasted_iota(jnp.int32, flat, 1) % SB_HEADS
                == lax.broadcasted_iota(jnp.int32, flat, 0) % SB_HEADS)
    for p in range(npg):
        a_p = jnp.where(own_head, a_spread[p * n_rows:(p + 1) * n_rows], 0.0).astype(BF16)
        acc = acc + _dot(a_p, v_refs[p][...].astype(BF16))

    for t in range(dec_seq):
        o_ref[0, t:t + 1, :] = jnp.concatenate(
            [acc[t * SB_HEADS + h:t * SB_HEADS + h + 1, :] for h in range(SB_HEADS)], axis=1)


def _sb_sample(page_table, q, k_new, v_new, cache_k, cache_v, u2, bias_rows):
    db, dec_seq, _ = q.shape
    npg = page_table.shape[1]
    n_rows = dec_seq * SB_HEADS
    page_rows = PAGE_SIZE * SB_HEADS
    pt_flat = page_table.reshape(-1)
    spread = (jnp.arange(page_rows)[None, :] // SB_HEADS == jnp.arange(PAGE_SIZE)[:, None]).astype(BF16)

    def page_spec(s):
        return pl.BlockSpec((page_rows, SB_HD), lambda b, pt: (pt[b * npg + s], 0))

    small = pl.BlockSpec((1, dec_seq, SB_WIDTH), lambda b, pt: (b, 0, 0))
    grid_spec = pltpu.PrefetchScalarGridSpec(
        num_scalar_prefetch=1,
        grid=(db,),
        in_specs=[small, small, small]
        + [page_spec(s) for s in range(npg)]
        + [page_spec(s) for s in range(npg)]
        + [pl.BlockSpec((2 * LANE, LANE), lambda b, pt: (0, 0)),
           pl.BlockSpec((n_rows, LANE), lambda b, pt: (0, 0)),
           pl.BlockSpec((PAGE_SIZE, page_rows), lambda b, pt: (0, 0))],
        out_specs=pl.BlockSpec((1, dec_seq, SB_WIDTH), lambda b, pt: (b, 0, 0)),
        scratch_shapes=[pltpu.VMEM((npg * PAGE_SIZE, SB_WIDTH), BF16)],
    )
    kern = functools.partial(_sb_sample_kernel, dec_seq=dec_seq, npg=npg)
    return pl.pallas_call(
        kern,
        grid_spec=grid_spec,
        out_shape=jax.ShapeDtypeStruct((db, dec_seq, SB_WIDTH), F32),
        compiler_params=pltpu.CompilerParams(
            dimension_semantics=("arbitrary",), vmem_limit_bytes=VMEM_LIMIT),
        name="sb_sample",
    )(pt_flat, q, k_new, v_new, *([cache_k] * npg), *([cache_v] * npg), u2, bias_rows, spread)


def _out_proj_kernel(osb_ref, sg_ref, ogla_ref, gg_ref, x_ref, gng_ref, w_ref, png_ref, y_ref):
    m1 = osb_ref[...] * jax.nn.silu(sg_ref[...])
    og = ogla_ref[...]
    gng = gng_ref[...]
    normed = []
    for h in range(GLA_HEADS):
        blk = og[:, h * GLA_DV:(h + 1) * GLA_DV]
        var = jnp.mean(blk * blk, axis=-1, keepdims=True)
        normed.append(blk * lax.rsqrt(var + EPS) * gng)
    m2 = jnp.concatenate(normed, axis=1) * jax.nn.silu(gg_ref[...])
    mixed = jnp.concatenate([m1, m2], axis=1).astype(BF16)
    br = _dot(mixed, w_ref[...])
    var = jnp.mean(br * br, axis=-1, keepdims=True)
    y_ref[...] = x_ref[...] + br * lax.rsqrt(var + EPS) * png_ref[...]


def _out_proj(o_sb, u, o_gla, x, gng, w_out, png, *, tm):
    m, d = x.shape
    assert m % tm == 0
    sgb = COL_SG // SB_WIDTH
    ggb = COL_GG // GLA_WIDTH
    mix = SB_WIDTH + GLA_WIDTH
    return pl.pallas_call(
        _out_proj_kernel,
        grid=(m // tm,),
        in_specs=[
            pl.BlockSpec((tm, SB_WIDTH), lambda i: (i, 0)),
            pl.BlockSpec((tm, SB_WIDTH), lambda i: (i, sgb)),
            pl.BlockSpec((tm, GLA_WIDTH), lambda i: (i, 0)),
            pl.BlockSpec((tm, GLA_WIDTH), lambda i: (i, ggb)),
            pl.BlockSpec((tm, d), lambda i: (i, 0)),
            pl.BlockSpec((1, GLA_DV), lambda i: (0, 0)),
            pl.BlockSpec((mix, d), lambda i: (0, 0)),
            pl.BlockSpec((1, d), lambda i: (0, 0)),
        ],
        out_specs=pl.BlockSpec((tm, d), lambda i: (i, 0)),
        out_shape=jax.ShapeDtypeStruct((m, d), F32),
        compiler_params=pltpu.CompilerParams(
            dimension_semantics=("arbitrary",), vmem_limit_bytes=VMEM_LIMIT),
        name="out_proj",
    )(o_sb, u, o_gla, u, x, gng, w_out, png)


def _tri2(n):
    r = jnp.arange(n)[:, None]
    c = jnp.arange(n)[None, :]
    u = (c < r).astype(BF16)
    return jnp.concatenate([u, u], axis=0)


def _seg_tri(block):
    r = jnp.arange(LANE)[:, None]
    c = jnp.arange(LANE)[None, :]
    return ((c <= r) & (c // block == r // block)).astype(BF16)


def _largest_tile(m, cap, quantum=SUBLANE):
    for t in range(min(cap, m) // quantum * quantum, 0, -quantum):
        if m % t == 0:
            return t
    raise ValueError(f"no {quantum}-aligned tile divides {m}")


def kernel(x_prompt, x_sample, cache_k, cache_v, state_gla, page_table, meta_tokens,
           norm_pre_g, w_in, sb_bias, w_alpha, b_alpha, gla_norm_g, w_out, norm_post_g):
    n_batch, seq, d = x_prompt.shape
    db, dec_seq, _ = x_sample.shape
    assert w_in.shape[0] == 1 and GLA_STEP % dec_seq == 0 and seq % GLA_STEP == 0
    n_prompt = n_batch * seq
    n_sample = db * dec_seq
    xp = x_prompt.reshape(n_prompt, d)
    assert n_sample % LANE == 0
    xs = jnp.concatenate([x_sample.reshape(n_sample, d), meta_tokens,
                          jnp.zeros((LANE - N_META, d), F32)], axis=0)
    meta_row = n_sample

    w_main = w_in[0, :, :N_MAIN].astype(BF16)
    w_ga = jnp.pad(w_in[0, :, N_MAIN:], ((0, 0), (0, LANE - GLA_RANK))).astype(BF16)
    w_al = jnp.pad(w_alpha[0], ((0, LANE - GLA_RANK), (0, 0))).astype(BF16)
    proj = functools.partial(_in_proj, g_pre=norm_pre_g, w_main=w_main, w_ga=w_ga, w_al=w_al,
                             b_al=b_alpha, tn=SB_WIDTH)
    step_seg = _seg_tri(GLA_STEP)
    u_p, b_p, k_flat, v_flat = proj(xp, seg=jnp.stack([step_seg, step_seg]),
                                    tm=_largest_tile(seq, 512, LANE), cache_rows=(seq, N_META))
    u_s, b_s = proj(xs, seg=jnp.stack([_seg_tri(dec_seq), step_seg]), tm=n_sample + LANE,
                    n_first=n_sample // LANE)

    w_out_bf = w_out[0].astype(BF16)

    tq = 256
    u2 = _tri2(LANE)
    o_sb_p = _sb_prompt(u_p, u_s, sb_bias[0], u2, n_batch=n_batch, seq=seq, tq=tq, hp=4,
                        meta_row=meta_row)
    o_gla_p, s_end = _gla_prompt(u_p.reshape(n_batch, seq, N_MAIN), b_p.reshape(n_batch, seq, GLA_KEY),
                                 u_s, b_s, n_batch=n_batch, seq=seq, tb=_largest_tile(seq, 128),
                                 bb=_largest_tile(n_batch, 8, 1), meta_row=meta_row)
    y_prompt = _out_proj(o_sb_p, u_p, o_gla_p.reshape(n_prompt, GLA_WIDTH), xp, gla_norm_g, w_out_bf,
                         norm_post_g, tm=_largest_tile(n_prompt, 256)).reshape(n_batch, seq, d)
    k_flat, v_flat = _meta_fill(k_flat, v_flat, u_s, n_seq=n_batch, meta_row=meta_row)
    cache_shape = (1, n_batch, N_META + seq, SB_HEADS, SB_HD)
    new_k_prompt = k_flat.reshape(cache_shape)
    new_v_prompt = v_flat.reshape(cache_shape)

    sample_cols = lambda t, col, width: t[:n_sample, col:col + width].reshape(db, dec_seq, width)
    q_s = sample_cols(u_s, COL_SQ, SB_WIDTH)
    k_s = sample_cols(u_s, COL_SK, SB_WIDTH)
    v_s = sample_cols(u_s, COL_SV, SB_WIDTH)
    n_pool = cache_k.shape[1]
    ck = cache_k.reshape(n_pool * PAGE_SIZE * SB_HEADS, SB_HD)
    cv = cache_v.reshape(n_pool * PAGE_SIZE * SB_HEADS, SB_HD)
    bias_rows = jnp.broadcast_to(jnp.tile(sb_bias[0], dec_seq)[:, None], (dec_seq * SB_HEADS, LANE))
    o_sb_s = _sb_sample(page_table, q_s, k_s, v_s, ck, cv, u2, bias_rows)

    pad_rows = lambda t: jnp.pad(t, ((0, 0), (0, GLA_STEP - dec_seq), (0, 0)))
    o_gla_s, s_new = _gla_sample(
        pad_rows(sample_cols(u_s, COL_GQ, GLA_KEY)), pad_rows(sample_cols(u_s, COL_GK, GLA_KEY)),
        pad_rows(sample_cols(u_s, COL_GV, GLA_WIDTH)), pad_rows(sample_cols(b_s, 0, GLA_KEY)),
        state_gla[0], bb=_largest_tile(db, 8, 1), dec_seq=dec_seq)
    y_sample = _out_proj(o_sb_s.reshape(n_sample, SB_WIDTH), u_s,
                         o_gla_s[:, :dec_seq].reshape(n_sample, GLA_WIDTH), x_sample.reshape(n_sample, d),
                         gla_norm_g, w_out_bf, norm_post_g,
                         tm=_largest_tile(n_sample, 256)).reshape(db, dec_seq, d)

    new_k_sample = k_s.reshape(1, db, dec_seq, SB_HEADS, SB_HD)
    new_v_sample = v_s.reshape(1, db, dec_seq, SB_HEADS, SB_HD)
    return (y_prompt, y_sample, new_k_prompt, new_v_prompt, s_end[None],
            new_k_sample, new_v_sample, s_new[None])
```

```python
import functools
import math

import jax
import jax.numpy as jnp
from jax import lax
from jax.experimental import pallas as pl
from jax.experimental.pallas import tpu as pltpu

F32 = jnp.float32
BF16 = jnp.bfloat16

N_META = 16
SB_HEADS = 8
SB_HD = 128
GLA_HEADS = 4
GLA_DK = 128
GLA_DV = 256
GLA_RANK = 16
GLA_TAU = 16.0
PAGE_SIZE = 128
EPS = 1e-6

LANE = 128
SUBLANE = 8
VMEM_LIMIT = 56 * 1024 * 1024
LOG2E = math.log2(math.e)

SB_WIDTH = SB_HEADS * SB_HD
GLA_KEY = GLA_HEADS * GLA_DK
GLA_WIDTH = GLA_HEADS * GLA_DV
N_MAIN = 4 * SB_WIDTH + 2 * GLA_KEY + 2 * GLA_WIDTH
COL_SQ, COL_SK, COL_SV, COL_SG = 0, SB_WIDTH, 2 * SB_WIDTH, 3 * SB_WIDTH
COL_GQ = 4 * SB_WIDTH
COL_GK = COL_GQ + GLA_KEY
COL_GV = COL_GK + GLA_KEY
COL_GG = COL_GV + GLA_WIDTH

GLA_STEP = 2 * SUBLANE
assert N_META == GLA_STEP


def _log_sigmoid(z):
    return jnp.minimum(z, 0.0) - jnp.log1p(jnp.exp(-jnp.abs(z)))


def _split_bf16(x):
    hi = x.astype(BF16)
    lo = (x - hi.astype(F32)).astype(BF16)
    return hi, lo


def _dot(a, b):
    return jnp.dot(a, b, preferred_element_type=F32)


def _dot_nt(a, b):
    return lax.dot_general(a, b, (((1,), (1,)), ((), ())), preferred_element_type=F32)


def _dot_tn(a, b):
    return lax.dot_general(a, b, (((0,), (0,)), ((), ())), preferred_element_type=F32)


def _in_proj_kernel(x_ref, g_ref, w_ref, wga_ref, wal_ref, bal_ref, seg_ref, u_ref, lf_ref, *rest,
                    n_first, cache_cols):
    h_scr = rest[-1]
    j = pl.program_id(1)

    @pl.when(j == 0)
    def _():
        x = x_ref[...]
        var = jnp.mean(x * x, axis=-1, keepdims=True)
        h = (x * lax.rsqrt(var + EPS) * g_ref[...]).astype(BF16)
        h_scr[...] = h
        ga = _dot(h, wga_ref[...])
        pre = _dot(ga.astype(BF16), wal_ref[...]) + bal_ref[...]
        g_hi, g_lo = _split_bf16(_log_sigmoid(pre) / GLA_TAU)
        for r in range(x.shape[0] // LANE):
            rows = slice(r * LANE, (r + 1) * LANE)
            seg = seg_ref[0 if r < n_first else 1]
            lf_ref[rows, :] = _dot(seg, g_hi[rows]) + _dot(seg, g_lo[rows])

    res = _dot(h_scr[...], w_ref[...])
    u_ref[...] = res.astype(u_ref.dtype)
    for ref, col in zip(rest[:-1], cache_cols):
        @pl.when(j == col)
        def _(ref=ref):
            for h in range(SB_HEADS):
                ref[pl.ds(h, res.shape[0], stride=SB_HEADS), :] = res[:, h * SB_HD:(h + 1) * SB_HD]


def _in_proj(x, g_pre, w_main, w_ga, w_al, b_al, seg, *, tm, tn, u_dtype, n_first=None, cache_rows=None):
    m, d = x.shape
    n = w_main.shape[1]
    assert m % tm == 0 and n % tn == 0 and tm % LANE == 0
    n_first = tm // LANE if n_first is None else n_first
    out_specs = [
        pl.BlockSpec((tm, tn), lambda i, j: (i, j)),
        pl.BlockSpec((tm, GLA_KEY), lambda i, j: (i, 0)),
    ]
    out_shape = [
        jax.ShapeDtypeStruct((m, n), u_dtype),
        jax.ShapeDtypeStruct((m, GLA_KEY), F32),
    ]
    cache_cols = ()
    if cache_rows is not None:
        seq, lead = cache_rows
        assert tn == SB_WIDTH and seq % tm == 0 and m % seq == 0
        per_seq = seq // tm
        cache_cols = (COL_SK // tn, COL_SV // tn)

        def cache_index(i, j):
            return (((i // per_seq) * (seq + lead) + lead + (i % per_seq) * tm) * SB_HEADS, 0)

        flat = jax.ShapeDtypeStruct(((m // seq) * (seq + lead) * SB_HEADS, SB_HD), F32)
        for _ in cache_cols:
            out_specs.append(pl.BlockSpec((pl.Element(tm * SB_HEADS), pl.Element(SB_HD)), cache_index,
                                          pipeline_mode=pl.Buffered(1)))
            out_shape.append(flat)
    kern = functools.partial(_in_proj_kernel, n_first=n_first, cache_cols=cache_cols)
    return pl.pallas_call(
        kern,
        grid=(m // tm, n // tn),
        in_specs=[
            pl.BlockSpec((tm, d), lambda i, j: (i, 0)),
            pl.BlockSpec((1, d), lambda i, j: (0, 0)),
            pl.BlockSpec((d, tn), lambda i, j: (0, j)),
            pl.BlockSpec((d, LANE), lambda i, j: (0, 0)),
            pl.BlockSpec((LANE, GLA_KEY), lambda i, j: (0, 0)),
            pl.BlockSpec((1, GLA_KEY), lambda i, j: (0, 0)),
            pl.BlockSpec((2, LANE, LANE), lambda i, j: (0, 0, 0)),
        ],
        out_specs=out_specs,
        out_shape=out_shape,
        scratch_shapes=[pltpu.VMEM((tm, d), BF16)],
        compiler_params=pltpu.CompilerParams(
            dimension_semantics=("arbitrary", "arbitrary"), vmem_limit_bytes=VMEM_LIMIT),
        name="in_proj",
    )(x, g_pre, w_main, w_ga, w_al, b_al, seg)


def _meta_fill_kernel(k_any, v_any, mk_ref, mv_ref, ko_ref, vo_ref):
    del k_any, v_any
    for ref, src in ((ko_ref, mk_ref), (vo_ref, mv_ref)):
        for h in range(SB_HEADS):
            ref[pl.ds(h, N_META, stride=SB_HEADS), :] = src[:N_META, h * SB_HD:(h + 1) * SB_HD]


def _meta_fill(k_flat, v_flat, u_m, *, n_seq, meta_row):
    rows = N_META * SB_HEADS
    per_seq = k_flat.shape[0] // n_seq
    assert per_seq % rows == 0
    blocks = per_seq // rows
    out = pl.BlockSpec((rows, SB_HD), lambda b: (b * blocks, 0))
    return pl.pallas_call(
        _meta_fill_kernel,
        grid=(n_seq,),
        in_specs=[
            pl.BlockSpec(memory_space=pl.ANY),
            pl.BlockSpec(memory_space=pl.ANY),
            pl.BlockSpec((LANE, SB_WIDTH), lambda b: (meta_row // LANE, COL_SK // SB_WIDTH)),
            pl.BlockSpec((LANE, SB_WIDTH), lambda b: (meta_row // LANE, COL_SV // SB_WIDTH)),
        ],
        out_specs=[out, out],
        out_shape=[jax.ShapeDtypeStruct(k_flat.shape, F32), jax.ShapeDtypeStruct(v_flat.shape, F32)],
        input_output_aliases={0: 0, 1: 1},
        compiler_params=pltpu.CompilerParams(dimension_semantics=("arbitrary",)),
        name="meta_fill",
    )(k_flat, v_flat, u_m, u_m)


SB_ZSCALE2 = (SB_HD ** -0.5) * LOG2E
SB_LAG = 2


def _softplus2(z_raw, zbias2):
    z2 = z_raw * SB_ZSCALE2 + zbias2
    t2 = jnp.log(1.0 + jnp.exp2(-jnp.abs(z2))) * LOG2E
    sp2 = jnp.maximum(z2, 0.0) + t2
    return sp2, z2 - sp2


def _tri_cumsum(sp2, u2):
    hi, lo = _split_bf16(sp2)
    return _dot(jnp.concatenate([hi, lo], axis=1), u2)


def _sb_scores(z_raw, zbias2, mask, u2):
    sp2, ls2 = _softplus2(z_raw, zbias2)
    if mask is not None:
        sp2 = jnp.where(mask, sp2, 0.0)
    subs = [sp2[:, j * LANE:(j + 1) * LANE] for j in range(sp2.shape[1] // LANE)]
    loc = _tri_cumsum(jnp.concatenate(subs, axis=0), u2)
    return ls2, loc, [jnp.sum(s, axis=-1, keepdims=True) for s in subs]


def _sb_weights(scores, mask, carry):
    ls2, loc, sums = scores
    rows = ls2.shape[0]
    later = [None] * len(sums)
    for j in range(len(sums) - 1, -1, -1):
        later[j] = loc[j * rows:(j + 1) * rows] + carry
        carry = carry + sums[j]
    a = jnp.exp2(ls2 - jnp.concatenate(later, axis=1))
    if mask is not None:
        a = jnp.where(mask, a, 0.0)
    return a.astype(BF16), carry


def _sb_streams(streams, u2):
    n_blocks = len(streams[0][2])
    order = [(s, b) for b in range(n_blocks) for s in range(len(streams))]
    zs = {(s, b): _dot_nt(streams[s][0], streams[s][2][b][0]) for s, b in order}
    acc = [st[3] for st in streams]
    carry = [st[4] for st in streams]
    scores = {}
    for n in range(len(order) + SB_LAG):
        if n < len(order):
            s, b = order[n]
            scores[s, b] = _sb_scores(zs[s, b], streams[s][1], streams[s][2][b][2], u2)
        if n >= SB_LAG:
            s, b = order[n - SB_LAG]
            _, v, m = streams[s][2][b]
            a, carry[s] = _sb_weights(scores.pop((s, b)), m, carry[s])
            acc[s] = acc[s] + _dot(a, v)
    return list(zip(acc, carry))


def _sb_prompt_kernel(bias_ref, q_ref, k_ref, v_ref, km_ref, vm_ref, u2_ref, o_ref, *, tq, hp):
    hb = pl.program_id(1)
    i = pl.program_id(2)
    u2 = u2_ref[...]
    cols = [slice(j * SB_HD, (j + 1) * SB_HD) for j in range(hp)]
    zb = [bias_ref[hb * hp + j] * LOG2E for j in range(hp)]
    q = [q_ref[:, c].astype(BF16) for c in cols]

    def kv(chunk, c):
        rows = pl.ds(pl.multiple_of(chunk * tq, tq), tq)
        return k_ref[rows, c].astype(BF16), v_ref[rows, c].astype(BF16)

    zero_carry = jnp.zeros((tq, 1), F32)
    mmask = lax.broadcasted_iota(jnp.int32, (tq, LANE), 1) < N_META
    z_meta = [_dot_nt(q[j], km_ref[:, cols[j]].astype(BF16)) for j in range(hp)]
    a_meta = [_sb_weights(_sb_scores(z_meta[j], zb[j], mmask, u2), mmask, zero_carry)[0]
              for j in range(hp)]
    pv_meta = [_dot(a_meta[j], vm_ref[:, cols[j]].astype(BF16)) for j in range(hp)]

    tri = lax.broadcasted_iota(jnp.int32, (tq, tq), 1) < lax.broadcasted_iota(jnp.int32, (tq, tq), 0)

    def pair(p, state):
        c0 = i - 2 * p
        first = jnp.logical_or(tri, p > 0)
        return tuple(_sb_streams(
            [(q[j], zb[j], [kv(c0, cols[j]) + (first,), kv(c0 - 1, cols[j]) + (None,)], *state[j])
             for j in range(hp)], u2))

    def single(_, state):
        only = jnp.logical_or(tri, i > 0)
        return tuple(_sb_streams(
            [(q[j], zb[j], [kv(0, cols[j]) + (only,)], *state[j]) for j in range(hp)], u2))

    state = tuple((jnp.zeros((tq, SB_HD), F32), zero_carry) for _ in range(hp))
    state = lax.fori_loop(0, lax.shift_right_logical(i + 1, 1), pair, state)
    state = lax.fori_loop(0, lax.bitwise_and(i + 1, 1), single, state)

    for j in range(hp):
        acc, carry = state[j]
        o_ref[:, cols[j]] = acc + pv_meta[j] * jnp.exp2(-carry)


def _sb_prompt(u_p, u_m, sb_bias, u2, *, n_batch, seq, tq, hp, meta_row):
    nq = seq // tq
    w = hp * SB_HD
    qb = COL_SQ // w
    kb = COL_SK // w
    vb = COL_SV // w
    kern = functools.partial(_sb_prompt_kernel, tq=tq, hp=hp)
    return pl.pallas_call(
        kern,
        grid=(n_batch, SB_HEADS // hp, nq),
        in_specs=[
            pl.BlockSpec(memory_space=pltpu.SMEM),
            pl.BlockSpec((tq, w), lambda b, h, i: (b * nq + i, qb + h)),
            pl.BlockSpec((seq, w), lambda b, h, i: (b, kb + h)),
            pl.BlockSpec((seq, w), lambda b, h, i: (b, vb + h)),
            pl.BlockSpec((LANE, w), lambda b, h, i: (meta_row // LANE, kb + h)),
            pl.BlockSpec((LANE, w), lambda b, h, i: (meta_row // LANE, vb + h)),
            pl.BlockSpec((2 * LANE, LANE), lambda b, h, i: (0, 0)),
        ],
        out_specs=pl.BlockSpec((tq, w), lambda b, h, i: (b * nq + i, h)),
        out_shape=jax.ShapeDtypeStruct((n_batch * seq, SB_WIDTH), F32),
        compiler_params=pltpu.CompilerParams(
            dimension_semantics=("arbitrary", "arbitrary", "arbitrary"), vmem_limit_bytes=VMEM_LIMIT),
        name="sb_prompt",
    )(sb_bias, u_p, u_p, u_p, u_m, u_m, u2)


def _gla_steps(chains, t_last):
    t_len = chains[0][1].shape[0]
    lane = lax.broadcasted_iota(jnp.int32, (t_len, t_len), 1)
    row = lax.broadcasted_iota(jnp.int32, (t_len, t_len), 0)
    prep = []
    for st_prev, q, k, v, b in chains:
        b_last = b[t_last:t_last + 1, :]
        pieces = []
        for s in range(t_len):
            skip = s // SUBLANE * SUBLANE
            decayed = q[skip:] * jnp.exp(jnp.minimum(b[skip:] - b[s:s + 1, :], 0.0))
            if skip:
                decayed = jnp.concatenate([jnp.zeros((skip, decayed.shape[1]), F32), decayed], axis=0)
            pieces.append(decayed.astype(BF16))
        lhs = jnp.concatenate(pieces, axis=0)
        qe = (q * jnp.exp(b)).astype(BF16)
        kd = (k.astype(F32) * jnp.exp(b_last - b)).astype(BF16)
        prep.append((lhs, qe, kd, k.astype(BF16), v.astype(BF16), jnp.exp(b_last)))
    mm = [(_dot_nt(lhs, k_bf), _dot_nt(qe, st_prev.astype(BF16)), _dot_tn(v_bf, kd))
          for (lhs, qe, kd, k_bf, v_bf, _), (st_prev, *_) in zip(prep, chains)]
    attns = []
    for r, _, _ in mm:
        attn = jnp.zeros((t_len, t_len), F32)
        for s in range(t_len):
            attn = jnp.where(lane == s, r[s * t_len:(s + 1) * t_len, :], attn)
        attns.append(jnp.where(lane <= row, attn, 0.0).astype(BF16))
    out = []
    for attn, (_, o_state, inc), (_, _, _, _, v_bf, dec), (st_prev, *_) in zip(attns, mm, prep, chains):
        out.append((st_prev * dec + inc, _dot(attn, v_bf) + o_state))
    return out


def _head_cols(h):
    return slice(h * GLA_DK, (h + 1) * GLA_DK), slice(h * GLA_DV, (h + 1) * GLA_DV)


def _gla_prompt_kernel(q_ref, k_ref, v_ref, b_ref, qm_ref, km_ref, vm_ref, bm_ref,
                       o_ref, s_ref, st_scr, *, n_steps, bb):
    blk = pl.program_id(1)
    scale = GLA_DK ** -0.5

    @pl.when(blk == 0)
    def _():
        chains = []
        for h in range(GLA_HEADS):
            kc, vc = _head_cols(h)
            chains.append((jnp.zeros((GLA_DV, GLA_DK), F32), qm_ref[:, kc].astype(F32) * scale, km_ref[:, kc],
                           vm_ref[:, vc], bm_ref[:, kc]))
        for h, (st1, _) in enumerate(_gla_steps(chains, GLA_STEP - 1)):
            for n in range(bb):
                st_scr[n, h] = st1

    def body(step, _):
        rows = pl.ds(pl.multiple_of(step * GLA_STEP, GLA_STEP), GLA_STEP)
        chains = []
        for n in range(bb):
            for h in range(GLA_HEADS):
                kc, vc = _head_cols(h)
                chains.append((st_scr[n, h], q_ref[n, rows, kc].astype(F32) * scale, k_ref[n, rows, kc],
                               v_ref[n, rows, vc], b_ref[n, rows, kc]))
        res = _gla_steps(chains, GLA_STEP - 1)
        for n in range(bb):
            for h in range(GLA_HEADS):
                st_scr[n, h] = res[n * GLA_HEADS + h][0]
            o_ref[n, rows, :] = jnp.concatenate(
                [res[n * GLA_HEADS + h][1] for h in range(GLA_HEADS)], axis=1)
        return 0

    lax.fori_loop(0, n_steps, body, 0)

    @pl.when(blk == pl.num_programs(1) - 1)
    def _():
        for n in range(bb):
            for h in range(GLA_HEADS):
                s_ref[n, h] = st_scr[n, h].T


def _gla_prompt(u_p, b_p, u_m, b_m, *, n_batch, seq, tb, bb, meta_row):
    qb = COL_GQ // GLA_KEY
    kb = COL_GK // GLA_KEY
    vb = COL_GV // GLA_WIDTH
    assert n_batch % bb == 0 and seq % tb == 0
    kern = functools.partial(_gla_prompt_kernel, n_steps=tb // GLA_STEP, bb=bb)
    return pl.pallas_call(
        kern,
        grid=(n_batch // bb, seq // tb),
        in_specs=[
            pl.BlockSpec((bb, tb, GLA_KEY), lambda n, j: (n, j, qb)),
            pl.BlockSpec((bb, tb, GLA_KEY), lambda n, j: (n, j, kb)),
            pl.BlockSpec((bb, tb, GLA_WIDTH), lambda n, j: (n, j, vb)),
            pl.BlockSpec((bb, tb, GLA_KEY), lambda n, j: (n, j, 0)),
            pl.BlockSpec((N_META, GLA_KEY), lambda n, j: (meta_row // N_META, qb)),
            pl.BlockSpec((N_META, GLA_KEY), lambda n, j: (meta_row // N_META, kb)),
            pl.BlockSpec((N_META, GLA_WIDTH), lambda n, j: (meta_row // N_META, vb)),
            pl.BlockSpec((N_META, GLA_KEY), lambda n, j: (meta_row // N_META, 0)),
        ],
        out_specs=[
            pl.BlockSpec((bb, tb, GLA_WIDTH), lambda n, j: (n, j, 0)),
            pl.BlockSpec((bb, GLA_HEADS, GLA_DK, GLA_DV), lambda n, j: (n, 0, 0, 0)),
        ],
        out_shape=[
            jax.ShapeDtypeStruct((n_batch, seq, GLA_WIDTH), F32),
            jax.ShapeDtypeStruct((n_batch, GLA_HEADS, GLA_DK, GLA_DV), F32),
        ],
        scratch_shapes=[pltpu.VMEM((bb, GLA_HEADS, GLA_DV, GLA_DK), F32)],
        compiler_params=pltpu.CompilerParams(
            dimension_semantics=("arbitrary", "arbitrary"), vmem_limit_bytes=VMEM_LIMIT),
        name="gla_prompt",
    )(u_p, u_p, u_p, b_p, u_m, u_m, u_m, b_m)


def _gla_sample_kernel(q_ref, k_ref, v_ref, b_ref, s_in_ref, o_ref, s_out_ref, *, bb, dec_seq):
    scale = GLA_DK ** -0.5
    chains = []
    for n in range(bb):
        for h in range(GLA_HEADS):
            kc, vc = _head_cols(h)
            chains.append((s_in_ref[n, h].T, q_ref[n, :, kc].astype(F32) * scale, k_ref[n, :, kc],
                           v_ref[n, :, vc], b_ref[n, :, kc]))
    res = _gla_steps(chains, dec_seq - 1)
    for n in range(bb):
        for h in range(GLA_HEADS):
            s_out_ref[n, h] = res[n * GLA_HEADS + h][0].T
        o_ref[n] = jnp.concatenate([res[n * GLA_HEADS + h][1] for h in range(GLA_HEADS)], axis=1)


def _gla_sample(gq, gk, gv, gb, state, *, bb, dec_seq):
    db = gq.shape[0]
    assert db % bb == 0
    kern = functools.partial(_gla_sample_kernel, bb=bb, dec_seq=dec_seq)
    return pl.pallas_call(
        kern,
        grid=(db // bb,),
        in_specs=[
            pl.BlockSpec((bb, GLA_STEP, GLA_KEY), lambda n: (n, 0, 0)),
            pl.BlockSpec((bb, GLA_STEP, GLA_KEY), lambda n: (n, 0, 0)),
            pl.BlockSpec((bb, GLA_STEP, GLA_WIDTH), lambda n: (n, 0, 0)),
            pl.BlockSpec((bb, GLA_STEP, GLA_KEY), lambda n: (n, 0, 0)),
            pl.BlockSpec((bb, GLA_HEADS, GLA_DK, GLA_DV), lambda n: (n, 0, 0, 0)),
        ],
        out_specs=[
            pl.BlockSpec((bb, GLA_STEP, GLA_WIDTH), lambda n: (n, 0, 0)),
            pl.BlockSpec((bb, GLA_HEADS, GLA_DK, GLA_DV), lambda n: (n, 0, 0, 0)),
        ],
        out_shape=[
            jax.ShapeDtypeStruct((db, GLA_STEP, GLA_WIDTH), F32),
            jax.ShapeDtypeStruct((db, GLA_HEADS, GLA_DK, GLA_DV), F32),
        ],
        compiler_params=pltpu.CompilerParams(
            dimension_semantics=("arbitrary",), vmem_limit_bytes=VMEM_LIMIT),
        name="gla_sample",
    )(gq, gk, gv, gb, state)


def _head_mask():
    head_of_col = lax.broadcasted_iota(jnp.int32, (SB_HEADS, SB_WIDTH), 1) // SB_HD
    return head_of_col == lax.broadcasted_iota(jnp.int32, (SB_HEADS, SB_WIDTH), 0)


def _sb_sample_kernel(pt_ref, q_ref, kn_ref, vn_ref, *refs, dec_seq, npg):
    del pt_ref
    k_refs = refs[:npg]
    v_refs = refs[npg:2 * npg]
    u2_ref, bias_ref, spread_ref, o_ref, k2d_scr = refs[2 * npg:]
    n_rows = dec_seq * SB_HEADS
    u2 = u2_ref[...]
    zbias2 = bias_ref[...][:, :1] * LOG2E

    hmask = _head_mask()
    qbd = jnp.concatenate(
        [jnp.where(hmask, jnp.broadcast_to(q_ref[0, t:t + 1, :], (SB_HEADS, SB_WIDTH)), 0.0)
         for t in range(dec_seq)], axis=0)

    lane = lax.broadcasted_iota(jnp.int32, (n_rows, LANE), 1)
    t_of_row = lax.broadcasted_iota(jnp.int32, (n_rows, LANE), 0) // SB_HEADS
    z_new = jnp.zeros((n_rows, LANE), F32)
    for t in range(dec_seq):
        zc = jnp.sum(qbd * kn_ref[0, t:t + 1, :], axis=-1, keepdims=True)
        z_new = jnp.where(lane == t, zc, z_new)
    mask = lane < t_of_row
    a_new, carry = _sb_weights(_sb_scores(z_new, zbias2, mask, u2), mask, jnp.zeros((n_rows, 1), F32))
    a_new = a_new.astype(F32)
    acc = jnp.zeros((n_rows, SB_HD), F32)
    for t in range(dec_seq):
        v_heads = jnp.concatenate(
            [vn_ref[0, t:t + 1, h * SB_HD:(h + 1) * SB_HD] for h in range(SB_HEADS)], axis=0)
        acc = acc + a_new[:, t:t + 1] * jnp.concatenate([v_heads] * dec_seq, axis=0)

    for s in range(npg):
        for h in range(SB_HEADS):
            rows = slice(s * PAGE_SIZE, (s + 1) * PAGE_SIZE)
            cols = slice(h * SB_HD, (h + 1) * SB_HD)
            k2d_scr[rows, cols] = k_refs[s][pl.ds(h, PAGE_SIZE, stride=SB_HEADS), :].astype(BF16)
    scores = _sb_scores(_dot_nt(qbd.astype(BF16), k2d_scr[...]), zbias2, None, u2)
    a, _ = _sb_weights(scores, None, carry)

    a_pages = jnp.concatenate([a[:, p * PAGE_SIZE:(p + 1) * PAGE_SIZE] for p in range(npg)], axis=0)
    a_spread = _dot(a_pages, spread_ref[...])
    flat = (n_rows, PAGE_SIZE * SB_HEADS)
    own_head = (lax.broadcasted_iota(jnp.int32, flat, 1) % SB_HEADS
                == lax.broadcasted_iota(jnp.int32, flat, 0) % SB_HEADS)
    for p in range(npg):
        a_p = jnp.where(own_head, a_spread[p * n_rows:(p + 1) * n_rows], 0.0).astype(BF16)
        acc = acc + _dot(a_p, v_refs[p][...].astype(BF16))

    for t in range(dec_seq):
        o_ref[0, t:t + 1, :] = jnp.concatenate(
            [acc[t * SB_HEADS + h:t * SB_HEADS + h + 1, :] for h in range(SB_HEADS)], axis=1)


def _sb_sample(page_table, q, k_new, v_new, cache_k, cache_v, u2, bias_rows):
    db, dec_seq, _ = q.shape
    npg = page_table.shape[1]
    n_rows = dec_seq * SB_HEADS
    page_rows = PAGE_SIZE * SB_HEADS
    pt_flat = page_table.reshape(-1)
    spread = (jnp.arange(page_rows)[None, :] // SB_HEADS == jnp.arange(PAGE_SIZE)[:, None]).astype(BF16)

    def page_spec(s):
        return pl.BlockSpec((page_rows, SB_HD), lambda b, pt: (pt[b * npg + s], 0))

    small = pl.BlockSpec((1, dec_seq, SB_WIDTH), lambda b, pt: (b, 0, 0))
    grid_spec = pltpu.PrefetchScalarGridSpec(
        num_scalar_prefetch=1,
        grid=(db,),
        in_specs=[small, small, small]
        + [page_spec(s) for s in range(npg)]
        + [page_spec(s) for s in range(npg)]
        + [pl.BlockSpec((2 * LANE, LANE), lambda b, pt: (0, 0)),
           pl.BlockSpec((n_rows, LANE), lambda b, pt: (0, 0)),
           pl.BlockSpec((PAGE_SIZE, page_rows), lambda b, pt: (0, 0))],
        out_specs=pl.BlockSpec((1, dec_seq, SB_WIDTH), lambda b, pt: (b, 0, 0)),
        scratch_shapes=[pltpu.VMEM((npg * PAGE_SIZE, SB_WIDTH), BF16)],
    )
    kern = functools.partial(_sb_sample_kernel, dec_seq=dec_seq, npg=npg)
    return pl.pallas_call(
        kern,
        grid_spec=grid_spec,
        out_shape=jax.ShapeDtypeStruct((db, dec_seq, SB_WIDTH), F32),
        compiler_params=pltpu.CompilerParams(
            dimension_semantics=("arbitrary",), vmem_limit_bytes=VMEM_LIMIT),
        name="sb_sample",
    )(pt_flat, q, k_new, v_new, *([cache_k] * npg), *([cache_v] * npg), u2, bias_rows, spread)


def _out_proj_kernel(osb_ref, sg_ref, ogla_ref, gg_ref, x_ref, gng_ref, w_ref, png_ref, y_ref):
    m1 = osb_ref[...] * jax.nn.silu(sg_ref[...].astype(F32))
    og = ogla_ref[...]
    gng = gng_ref[...]
    normed = []
    for h in range(GLA_HEADS):
        blk = og[:, h * GLA_DV:(h + 1) * GLA_DV]
        var = jnp.mean(blk * blk, axis=-1, keepdims=True)
        normed.append(blk * lax.rsqrt(var + EPS) * gng)
    m2 = jnp.concatenate(normed, axis=1) * jax.nn.silu(gg_ref[...].astype(F32))
    mixed = jnp.concatenate([m1, m2], axis=1).astype(BF16)
    br = _dot(mixed, w_ref[...])
    var = jnp.mean(br * br, axis=-1, keepdims=True)
    y_ref[...] = x_ref[...] + br * lax.rsqrt(var + EPS) * png_ref[...]


def _out_proj(o_sb, u, o_gla, x, gng, w_out, png, *, tm):
    m, d = x.shape
    assert m % tm == 0
    sgb = COL_SG // SB_WIDTH
    ggb = COL_GG // GLA_WIDTH
    mix = SB_WIDTH + GLA_WIDTH
    return pl.pallas_call(
        _out_proj_kernel,
        grid=(m // tm,),
        in_specs=[
            pl.BlockSpec((tm, SB_WIDTH), lambda i: (i, 0)),
            pl.BlockSpec((tm, SB_WIDTH), lambda i: (i, sgb)),
            pl.BlockSpec((tm, GLA_WIDTH), lambda i: (i, 0)),
            pl.BlockSpec((tm, GLA_WIDTH), lambda i: (i, ggb)),
            pl.BlockSpec((tm, d), lambda i: (i, 0)),
            pl.BlockSpec((1, GLA_DV), lambda i: (0, 0)),
            pl.BlockSpec((mix, d), lambda i: (0, 0), pipeline_mode=pl.Buffered(1)),
            pl.BlockSpec((1, d), lambda i: (0, 0)),
        ],
        out_specs=pl.BlockSpec((tm, d), lambda i: (i, 0)),
        out_shape=jax.ShapeDtypeStruct((m, d), F32),
        compiler_params=pltpu.CompilerParams(
            dimension_semantics=("arbitrary",), vmem_limit_bytes=VMEM_LIMIT),
        name="out_proj",
    )(o_sb, u, o_gla, u, x, gng, w_out, png)


def _tri2(n):
    r = jnp.arange(n)[:, None]
    c = jnp.arange(n)[None, :]
    u = (c < r).astype(BF16)
    return jnp.concatenate([u, u], axis=0)


def _seg_tri(block):
    r = jnp.arange(LANE)[:, None]
    c = jnp.arange(LANE)[None, :]
    return ((c <= r) & (c // block == r // block)).astype(BF16)


def _largest_tile(m, cap, quantum=SUBLANE):
    for t in range(min(cap, m) // quantum * quantum, 0, -quantum):
        if m % t == 0:
            return t
    raise ValueError(f"no {quantum}-aligned tile divides {m}")


def kernel(x_prompt, x_sample, cache_k, cache_v, state_gla, page_table, meta_tokens,
           norm_pre_g, w_in, sb_bias, w_alpha, b_alpha, gla_norm_g, w_out, norm_post_g):
    n_batch, seq, d = x_prompt.shape
    db, dec_seq, _ = x_sample.shape
    assert w_in.shape[0] == 1 and GLA_STEP % dec_seq == 0 and seq % GLA_STEP == 0
    n_prompt = n_batch * seq
    n_sample = db * dec_seq
    xp = x_prompt.reshape(n_prompt, d)
    assert n_sample % LANE == 0
    xs = jnp.concatenate([x_sample.reshape(n_sample, d), meta_tokens,
                          jnp.zeros((LANE - N_META, d), F32)], axis=0)
    meta_row = n_sample

    w_main = w_in[0, :, :N_MAIN].astype(BF16)
    w_ga = jnp.pad(w_in[0, :, N_MAIN:], ((0, 0), (0, LANE - GLA_RANK))).astype(BF16)
    w_al = jnp.pad(w_alpha[0], ((0, LANE - GLA_RANK), (0, 0))).astype(BF16)
    proj = functools.partial(_in_proj, g_pre=norm_pre_g, w_main=w_main, w_ga=w_ga, w_al=w_al,
                             b_al=b_alpha, tn=SB_WIDTH)
    step_seg = _seg_tri(GLA_STEP)
    u_p, b_p, k_flat, v_flat = proj(xp, seg=jnp.stack([step_seg, step_seg]), u_dtype=BF16,
                                    tm=_largest_tile(seq, 1024, LANE), cache_rows=(seq, N_META))
    u_s, b_s = proj(xs, seg=jnp.stack([_seg_tri(dec_seq), step_seg]), u_dtype=F32,
                    tm=n_sample + LANE, n_first=n_sample // LANE)

    w_out_bf = w_out[0].astype(BF16)

    tq = 256
    u2 = _tri2(LANE)
    o_sb_p = _sb_prompt(u_p, u_s, sb_bias[0], u2, n_batch=n_batch, seq=seq, tq=tq, hp=4,
                        meta_row=meta_row)
    o_gla_p, s_end = _gla_prompt(u_p.reshape(n_batch, seq, N_MAIN), b_p.reshape(n_batch, seq, GLA_KEY),
                                 u_s, b_s, n_batch=n_batch, seq=seq, tb=_largest_tile(seq, 128),
                                 bb=_largest_tile(n_batch, 8, 1), meta_row=meta_row)
    y_prompt = _out_proj(o_sb_p, u_p, o_gla_p.reshape(n_prompt, GLA_WIDTH), xp, gla_norm_g, w_out_bf,
                         norm_post_g, tm=_largest_tile(n_prompt, 256)).reshape(n_batch, seq, d)
    k_flat, v_flat = _meta_fill(k_flat, v_flat, u_s, n_seq=n_batch, meta_row=meta_row)
    cache_shape = (1, n_batch, N_META + seq, SB_HEADS, SB_HD)
    new_k_prompt = k_flat.reshape(cache_shape)
    new_v_prompt = v_flat.reshape(cache_shape)

    sample_cols = lambda t, col, width: t[:n_sample, col:col + width].reshape(db, dec_seq, width)
    q_s = sample_cols(u_s, COL_SQ, SB_WIDTH)
    k_s = sample_cols(u_s, COL_SK, SB_WIDTH)
    v_s = sample_cols(u_s, COL_SV, SB_WIDTH)
    n_pool = cache_k.shape[1]
    ck = cache_k.reshape(n_pool * PAGE_SIZE * SB_HEADS, SB_HD)
    cv = cache_v.reshape(n_pool * PAGE_SIZE * SB_HEADS, SB_HD)
    bias_rows = jnp.broadcast_to(jnp.tile(sb_bias[0], dec_seq)[:, None], (dec_seq * SB_HEADS, LANE))
    o_sb_s = _sb_sample(page_table, q_s, k_s, v_s, ck, cv, u2, bias_rows)

    pad_rows = lambda t: jnp.pad(t, ((0, 0), (0, GLA_STEP - dec_seq), (0, 0)))
    o_gla_s, s_new = _gla_sample(
        pad_rows(sample_cols(u_s, COL_GQ, GLA_KEY)), pad_rows(sample_cols(u_s, COL_GK, GLA_KEY)),
        pad_rows(sample_cols(u_s, COL_GV, GLA_WIDTH)), pad_rows(sample_cols(b_s, 0, GLA_KEY)),
        state_gla[0], bb=_largest_tile(db, 8, 1), dec_seq=dec_seq)
    y_sample = _out_proj(o_sb_s.reshape(n_sample, SB_WIDTH), u_s,
                         o_gla_s[:, :dec_seq].reshape(n_sample, GLA_WIDTH), x_sample.reshape(n_sample, d),
                         gla_norm_g, w_out_bf, norm_post_g,
                         tm=_largest_tile(n_sample, 256)).reshape(db, dec_seq, d)

    new_k_sample = k_s.reshape(1, db, dec_seq, SB_HEADS, SB_HD)
    new_v_sample = v_s.reshape(1, db, dec_seq, SB_HEADS, SB_HD)
    return (y_prompt, y_sample, new_k_prompt, new_v_prompt, s_end[None],
            new_k_sample, new_v_sample, s_new[None])
```

```python
import functools
import math

import jax
import jax.numpy as jnp
from jax import lax
from jax.experimental import pallas as pl
from jax.experimental.pallas import tpu as pltpu

F32 = jnp.float32
BF16 = jnp.bfloat16

N_META = 16
SB_HEADS = 8
SB_HD = 128
GLA_HEADS = 4
GLA_DK = 128
GLA_DV = 256
GLA_RANK = 16
GLA_TAU = 16.0
PAGE_SIZE = 128
EPS = 1e-6

LANE = 128
SUBLANE = 8
VMEM_LIMIT = 56 * 1024 * 1024
LOG2E = math.log2(math.e)

SB_WIDTH = SB_HEADS * SB_HD
GLA_KEY = GLA_HEADS * GLA_DK
GLA_WIDTH = GLA_HEADS * GLA_DV
N_MAIN = 4 * SB_WIDTH + 2 * GLA_KEY + 2 * GLA_WIDTH
COL_SQ, COL_SK, COL_SV, COL_SG = 0, SB_WIDTH, 2 * SB_WIDTH, 3 * SB_WIDTH
COL_GQ = 4 * SB_WIDTH
COL_GK = COL_GQ + GLA_KEY
COL_GV = COL_GK + GLA_KEY
COL_GG = COL_GV + GLA_WIDTH

GLA_STEP = 2 * SUBLANE
assert N_META == GLA_STEP


def _log_sigmoid(z):
    return jnp.minimum(z, 0.0) - jnp.log(1.0 + jnp.exp(-jnp.abs(z)))


def _split_bf16(x):
    hi = x.astype(BF16)
    lo = (x - hi.astype(F32)).astype(BF16)
    return hi, lo


def _dot(a, b):
    return jnp.dot(a, b, preferred_element_type=F32)


def _dot_nt(a, b):
    return lax.dot_general(a, b, (((1,), (1,)), ((), ())), preferred_element_type=F32)


def _dot_tn(a, b):
    return lax.dot_general(a, b, (((0,), (0,)), ((), ())), preferred_element_type=F32)


def _in_proj_kernel(x_ref, g_ref, w_ref, wga_ref, wal_ref, bal_ref, seg_ref, u_ref, lf_ref, *rest,
                    n_first, cache_cols):
    h_scr = rest[-1]
    j = pl.program_id(1)

    @pl.when(j == 0)
    def _():
        x = x_ref[...]
        var = jnp.mean(x * x, axis=-1, keepdims=True)
        h = (x * lax.rsqrt(var + EPS) * g_ref[...]).astype(BF16)
        h_scr[...] = h
        ga = _dot(h, wga_ref[...])
        pre = _dot(ga.astype(BF16), wal_ref[...]) + bal_ref[...]
        g_hi, g_lo = _split_bf16(_log_sigmoid(pre) / GLA_TAU)
        for r in range(x.shape[0] // LANE):
            rows = slice(r * LANE, (r + 1) * LANE)
            seg = seg_ref[0 if r < n_first else 1]
            lf_ref[rows, :] = _dot(seg, g_hi[rows]) + _dot(seg, g_lo[rows])

    res = _dot(h_scr[...], w_ref[...])
    u_ref[...] = res.astype(u_ref.dtype)
    for ref, col in zip(rest[:-1], cache_cols):
        @pl.when(j == col)
        def _(ref=ref):
            for h in range(SB_HEADS):
                ref[pl.ds(h, res.shape[0], stride=SB_HEADS), :] = res[:, h * SB_HD:(h + 1) * SB_HD]


def _in_proj(x, g_pre, w_main, w_ga, w_al, b_al, seg, *, tm, tn, u_dtype, n_first=None, cache_rows=None):
    m, d = x.shape
    n = N_MAIN
    assert m % tm == 0 and n % tn == 0 and tm % LANE == 0 and w_main.shape[1] >= n
    n_first = tm // LANE if n_first is None else n_first
    out_specs = [
        pl.BlockSpec((tm, tn), lambda i, j: (i, j)),
        pl.BlockSpec((tm, GLA_KEY), lambda i, j: (i, 0)),
    ]
    out_shape = [
        jax.ShapeDtypeStruct((m, n), u_dtype),
        jax.ShapeDtypeStruct((m, GLA_KEY), F32),
    ]
    cache_cols = ()
    if cache_rows is not None:
        seq, lead = cache_rows
        assert tn == SB_WIDTH and seq % tm == 0 and m % seq == 0
        per_seq = seq // tm
        cache_cols = (COL_SK // tn, COL_SV // tn)

        def cache_index(i, j):
            return (((i // per_seq) * (seq + lead) + lead + (i % per_seq) * tm) * SB_HEADS, 0)

        flat = jax.ShapeDtypeStruct(((m // seq) * (seq + lead) * SB_HEADS, SB_HD), F32)
        for _ in cache_cols:
            out_specs.append(pl.BlockSpec((pl.Element(tm * SB_HEADS), pl.Element(SB_HD)), cache_index,
                                          pipeline_mode=pl.Buffered(1)))
            out_shape.append(flat)
    kern = functools.partial(_in_proj_kernel, n_first=n_first, cache_cols=cache_cols)
    return pl.pallas_call(
        kern,
        grid=(m // tm, n // tn),
        in_specs=[
            pl.BlockSpec((tm, d), lambda i, j: (i, 0)),
            pl.BlockSpec((1, d), lambda i, j: (0, 0)),
            pl.BlockSpec((d, tn), lambda i, j: (0, j)),
            pl.BlockSpec((d, LANE), lambda i, j: (0, 0)),
            pl.BlockSpec((LANE, GLA_KEY), lambda i, j: (0, 0)),
            pl.BlockSpec((1, GLA_KEY), lambda i, j: (0, 0)),
            pl.BlockSpec((2, LANE, LANE), lambda i, j: (0, 0, 0)),
        ],
        out_specs=out_specs,
        out_shape=out_shape,
        scratch_shapes=[pltpu.VMEM((tm, d), BF16)],
        compiler_params=pltpu.CompilerParams(
            dimension_semantics=("arbitrary", "arbitrary"), vmem_limit_bytes=VMEM_LIMIT),
        name="in_proj",
    )(x, g_pre, w_main, w_ga, w_al, b_al, seg)


def _meta_fill_kernel(k_any, v_any, mk_ref, mv_ref, ko_ref, vo_ref):
    del k_any, v_any
    for ref, src in ((ko_ref, mk_ref), (vo_ref, mv_ref)):
        for h in range(SB_HEADS):
            ref[pl.ds(h, N_META, stride=SB_HEADS), :] = src[:N_META, h * SB_HD:(h + 1) * SB_HD]


def _meta_fill(k_flat, v_flat, u_m, *, n_seq, meta_row):
    rows = N_META * SB_HEADS
    per_seq = k_flat.shape[0] // n_seq
    assert per_seq % rows == 0
    blocks = per_seq // rows
    out = pl.BlockSpec((rows, SB_HD), lambda b: (b * blocks, 0))
    return pl.pallas_call(
        _meta_fill_kernel,
        grid=(n_seq,),
        in_specs=[
            pl.BlockSpec(memory_space=pl.ANY),
            pl.BlockSpec(memory_space=pl.ANY),
            pl.BlockSpec((LANE, SB_WIDTH), lambda b: (meta_row // LANE, COL_SK // SB_WIDTH)),
            pl.BlockSpec((LANE, SB_WIDTH), lambda b: (meta_row // LANE, COL_SV // SB_WIDTH)),
        ],
        out_specs=[out, out],
        out_shape=[jax.ShapeDtypeStruct(k_flat.shape, F32), jax.ShapeDtypeStruct(v_flat.shape, F32)],
        input_output_aliases={0: 0, 1: 1},
        compiler_params=pltpu.CompilerParams(dimension_semantics=("arbitrary",)),
        name="meta_fill",
    )(k_flat, v_flat, u_m, u_m)


SB_ZSCALE2 = (SB_HD ** -0.5) * LOG2E
SB_LAG = 2


def _softplus2(z_raw, zbias2):
    z2 = z_raw * SB_ZSCALE2 + zbias2
    t2 = jnp.log(1.0 + jnp.exp2(-jnp.abs(z2))) * LOG2E
    sp2 = jnp.maximum(z2, 0.0) + t2
    return sp2, z2 - sp2


def _tri_cumsum(sp2, u2):
    return _dot(sp2.astype(BF16), u2)


def _sb_scores(z_raw, zbias2, mask, u2):
    sp2, ls2 = _softplus2(z_raw, zbias2)
    if mask is not None:
        sp2 = jnp.where(mask, sp2, 0.0)
    subs = [sp2[:, j * LANE:(j + 1) * LANE] for j in range(sp2.shape[1] // LANE)]
    loc = _tri_cumsum(jnp.concatenate(subs, axis=0), u2)
    return ls2, loc, [jnp.sum(s, axis=-1, keepdims=True) for s in subs]


def _sb_weights(scores, mask, carry):
    ls2, loc, sums = scores
    rows = ls2.shape[0]
    later = [None] * len(sums)
    for j in range(len(sums) - 1, -1, -1):
        later[j] = loc[j * rows:(j + 1) * rows] + carry
        carry = carry + sums[j]
    a = jnp.exp2(ls2 - jnp.concatenate(later, axis=1))
    if mask is not None:
        a = jnp.where(mask, a, 0.0)
    return a.astype(BF16), carry


def _sb_streams(streams, u2):
    n_blocks = len(streams[0][2])
    order = [(s, b) for b in range(n_blocks) for s in range(len(streams))]
    zs = {(s, b): _dot_nt(streams[s][0], streams[s][2][b][0]) for s, b in order}
    acc = [st[3] for st in streams]
    carry = [st[4] for st in streams]
    scores = {}
    for n in range(len(order) + SB_LAG):
        if n < len(order):
            s, b = order[n]
            scores[s, b] = _sb_scores(zs[s, b], streams[s][1], streams[s][2][b][2], u2)
        if n >= SB_LAG:
            s, b = order[n - SB_LAG]
            _, v, m = streams[s][2][b]
            a, carry[s] = _sb_weights(scores.pop((s, b)), m, carry[s])
            acc[s] = acc[s] + _dot(a, v)
    return list(zip(acc, carry))


def _sb_prompt_kernel(bias_ref, q_ref, k_ref, v_ref, km_ref, vm_ref, u2_ref, o_ref, *, tq, hp):
    hb = pl.program_id(1)
    i = pl.program_id(2)
    u2 = u2_ref[...]
    cols = [slice(j * SB_HD, (j + 1) * SB_HD) for j in range(hp)]
    zb = [bias_ref[hb * hp + j] * LOG2E for j in range(hp)]
    q = [q_ref[:, c].astype(BF16) for c in cols]

    def kv(chunk, c):
        rows = pl.ds(pl.multiple_of(chunk * tq, tq), tq)
        return k_ref[rows, c].astype(BF16), v_ref[rows, c].astype(BF16)

    zero_carry = jnp.zeros((tq, 1), F32)
    mmask = lax.broadcasted_iota(jnp.int32, (tq, LANE), 1) < N_META
    z_meta = [_dot_nt(q[j], km_ref[:, cols[j]].astype(BF16)) for j in range(hp)]
    a_meta = [_sb_weights(_sb_scores(z_meta[j], zb[j], mmask, u2), mmask, zero_carry)[0]
              for j in range(hp)]
    pv_meta = [_dot(a_meta[j], vm_ref[:, cols[j]].astype(BF16)) for j in range(hp)]

    tri = lax.broadcasted_iota(jnp.int32, (tq, tq), 1) < lax.broadcasted_iota(jnp.int32, (tq, tq), 0)

    def pair(p, state):
        c0 = i - 2 * p
        first = jnp.logical_or(tri, p > 0)
        return tuple(_sb_streams(
            [(q[j], zb[j], [kv(c0, cols[j]) + (first,), kv(c0 - 1, cols[j]) + (None,)], *state[j])
             for j in range(hp)], u2))

    def single(_, state):
        only = jnp.logical_or(tri, i > 0)
        return tuple(_sb_streams(
            [(q[j], zb[j], [kv(0, cols[j]) + (only,)], *state[j]) for j in range(hp)], u2))

    state = tuple((jnp.zeros((tq, SB_HD), F32), zero_carry) for _ in range(hp))
    state = lax.fori_loop(0, lax.shift_right_logical(i + 1, 1), pair, state)
    state = lax.fori_loop(0, lax.bitwise_and(i + 1, 1), single, state)

    for j in range(hp):
        acc, carry = state[j]
        o_ref[:, cols[j]] = acc + pv_meta[j] * jnp.exp2(-carry)


def _sb_prompt(u_p, u_m, sb_bias, u2, *, n_batch, seq, tq, hp, meta_row):
    nq = seq // tq
    w = hp * SB_HD
    qb = COL_SQ // w
    kb = COL_SK // w
    vb = COL_SV // w
    kern = functools.partial(_sb_prompt_kernel, tq=tq, hp=hp)
    return pl.pallas_call(
        kern,
        grid=(n_batch, SB_HEADS // hp, nq),
        in_specs=[
            pl.BlockSpec(memory_space=pltpu.SMEM),
            pl.BlockSpec((tq, w), lambda b, h, i: (b * nq + i, qb + h)),
            pl.BlockSpec((seq, w), lambda b, h, i: (b, kb + h)),
            pl.BlockSpec((seq, w), lambda b, h, i: (b, vb + h)),
            pl.BlockSpec((LANE, w), lambda b, h, i: (meta_row // LANE, kb + h)),
            pl.BlockSpec((LANE, w), lambda b, h, i: (meta_row // LANE, vb + h)),
            pl.BlockSpec((LANE, LANE), lambda b, h, i: (0, 0)),
        ],
        out_specs=pl.BlockSpec((tq, w), lambda b, h, i: (b * nq + i, h)),
        out_shape=jax.ShapeDtypeStruct((n_batch * seq, SB_WIDTH), F32),
        compiler_params=pltpu.CompilerParams(
            dimension_semantics=("arbitrary", "arbitrary", "arbitrary"), vmem_limit_bytes=VMEM_LIMIT),
        name="sb_prompt",
    )(sb_bias, u_p, u_p, u_p, u_m, u_m, u2)


def _gla_steps(chains, t_last):
    t_len = chains[0][1].shape[0]
    lane = lax.broadcasted_iota(jnp.int32, (t_len, t_len), 1)
    row = lax.broadcasted_iota(jnp.int32, (t_len, t_len), 0)
    prep = []
    for st_prev, q, k, v, b in chains:
        b_last = b[t_last:t_last + 1, :]
        pieces = []
        for s in range(t_len):
            skip = s // SUBLANE * SUBLANE
            decayed = q[skip:] * jnp.exp(jnp.minimum(b[skip:] - b[s:s + 1, :], 0.0))
            if skip:
                decayed = jnp.concatenate([jnp.zeros((skip, decayed.shape[1]), F32), decayed], axis=0)
            pieces.append(decayed.astype(BF16))
        lhs = jnp.concatenate(pieces, axis=0)
        qe = (q * jnp.exp(b)).astype(BF16)
        kd = (k.astype(F32) * jnp.exp(b_last - b)).astype(BF16)
        prep.append((lhs, qe, kd, k.astype(BF16), v.astype(BF16), jnp.exp(b_last)))
    mm = [(_dot_nt(lhs, k_bf), _dot_nt(qe, st_prev.astype(BF16)), _dot_tn(v_bf, kd))
          for (lhs, qe, kd, k_bf, v_bf, _), (st_prev, *_) in zip(prep, chains)]
    attns = []
    for r, _, _ in mm:
        attn = jnp.zeros((t_len, t_len), F32)
        for s in range(t_len):
            attn = jnp.where(lane == s, r[s * t_len:(s + 1) * t_len, :], attn)
        attns.append(jnp.where(lane <= row, attn, 0.0).astype(BF16))
    out = []
    for attn, (_, o_state, inc), (_, _, _, _, v_bf, dec), (st_prev, *_) in zip(attns, mm, prep, chains):
        out.append((st_prev * dec + inc, _dot(attn, v_bf) + o_state))
    return out


def _head_cols(h):
    return slice(h * GLA_DK, (h + 1) * GLA_DK), slice(h * GLA_DV, (h + 1) * GLA_DV)


def _gla_prompt_kernel(q_ref, k_ref, v_ref, b_ref, qm_ref, km_ref, vm_ref, bm_ref,
                       o_ref, s_ref, st_scr, *, n_steps, bb):
    blk = pl.program_id(1)
    scale = GLA_DK ** -0.5

    @pl.when(blk == 0)
    def _():
        chains = []
        for h in range(GLA_HEADS):
            kc, vc = _head_cols(h)
            chains.append((jnp.zeros((GLA_DV, GLA_DK), F32), qm_ref[:, kc].astype(F32) * scale, km_ref[:, kc],
                           vm_ref[:, vc], bm_ref[:, kc]))
        for h, (st1, _) in enumerate(_gla_steps(chains, GLA_STEP - 1)):
            for n in range(bb):
                st_scr[n, h] = st1

    def body(step, _):
        rows = pl.ds(pl.multiple_of(step * GLA_STEP, GLA_STEP), GLA_STEP)
        chains = []
        for n in range(bb):
            for h in range(GLA_HEADS):
                kc, vc = _head_cols(h)
                chains.append((st_scr[n, h], q_ref[n, rows, kc].astype(F32) * scale, k_ref[n, rows, kc],
                               v_ref[n, rows, vc], b_ref[n, rows, kc]))
        res = _gla_steps(chains, GLA_STEP - 1)
        for n in range(bb):
            for h in range(GLA_HEADS):
                st_scr[n, h] = res[n * GLA_HEADS + h][0]
            o_ref[n, rows, :] = jnp.concatenate(
                [res[n * GLA_HEADS + h][1] for h in range(GLA_HEADS)], axis=1)
        return 0

    lax.fori_loop(0, n_steps, body, 0)

    @pl.when(blk == pl.num_programs(1) - 1)
    def _():
        for n in range(bb):
            for h in range(GLA_HEADS):
                s_ref[n, h] = st_scr[n, h].T


def _gla_prompt(u_p, b_p, u_m, b_m, *, n_batch, seq, tb, bb, meta_row):
    qb = COL_GQ // GLA_KEY
    kb = COL_GK // GLA_KEY
    vb = COL_GV // GLA_WIDTH
    assert n_batch % bb == 0 and seq % tb == 0
    kern = functools.partial(_gla_prompt_kernel, n_steps=tb // GLA_STEP, bb=bb)
    return pl.pallas_call(
        kern,
        grid=(n_batch // bb, seq // tb),
        in_specs=[
            pl.BlockSpec((bb, tb, GLA_KEY), lambda n, j: (n, j, qb)),
            pl.BlockSpec((bb, tb, GLA_KEY), lambda n, j: (n, j, kb)),
            pl.BlockSpec((bb, tb, GLA_WIDTH), lambda n, j: (n, j, vb)),
            pl.BlockSpec((bb, tb, GLA_KEY), lambda n, j: (n, j, 0)),
            pl.BlockSpec((N_META, GLA_KEY), lambda n, j: (meta_row // N_META, qb)),
            pl.BlockSpec((N_META, GLA_KEY), lambda n, j: (meta_row // N_META, kb)),
            pl.BlockSpec((N_META, GLA_WIDTH), lambda n, j: (meta_row // N_META, vb)),
            pl.BlockSpec((N_META, GLA_KEY), lambda n, j: (meta_row // N_META, 0)),
        ],
        out_specs=[
            pl.BlockSpec((bb, tb, GLA_WIDTH), lambda n, j: (n, j, 0)),
            pl.BlockSpec((bb, GLA_HEADS, GLA_DK, GLA_DV), lambda n, j: (n, 0, 0, 0)),
        ],
        out_shape=[
            jax.ShapeDtypeStruct((n_batch, seq, GLA_WIDTH), F32),
            jax.ShapeDtypeStruct((n_batch, GLA_HEADS, GLA_DK, GLA_DV), F32),
        ],
        scratch_shapes=[pltpu.VMEM((bb, GLA_HEADS, GLA_DV, GLA_DK), F32)],
        compiler_params=pltpu.CompilerParams(
            dimension_semantics=("arbitrary", "arbitrary"), vmem_limit_bytes=VMEM_LIMIT),
        name="gla_prompt",
    )(u_p, u_p, u_p, b_p, u_m, u_m, u_m, b_m)


def _gla_sample_kernel(q_ref, k_ref, v_ref, b_ref, s_in_ref, o_ref, s_out_ref, *, bb, dec_seq):
    scale = GLA_DK ** -0.5
    chains = []
    for n in range(bb):
        for h in range(GLA_HEADS):
            kc, vc = _head_cols(h)
            chains.append((s_in_ref[n, h].T, q_ref[n, :, kc].astype(F32) * scale, k_ref[n, :, kc],
                           v_ref[n, :, vc], b_ref[n, :, kc]))
    res = _gla_steps(chains, dec_seq - 1)
    for n in range(bb):
        for h in range(GLA_HEADS):
            s_out_ref[n, h] = res[n * GLA_HEADS + h][0].T
        o_ref[n] = jnp.concatenate([res[n * GLA_HEADS + h][1] for h in range(GLA_HEADS)], axis=1)


def _gla_sample(gq, gk, gv, gb, state, *, bb, dec_seq):
    db = gq.shape[0]
    assert db % bb == 0
    kern = functools.partial(_gla_sample_kernel, bb=bb, dec_seq=dec_seq)
    return pl.pallas_call(
        kern,
        grid=(db // bb,),
        in_specs=[
            pl.BlockSpec((bb, GLA_STEP, GLA_KEY), lambda n: (n, 0, 0)),
            pl.BlockSpec((bb, GLA_STEP, GLA_KEY), lambda n: (n, 0, 0)),
            pl.BlockSpec((bb, GLA_STEP, GLA_WIDTH), lambda n: (n, 0, 0)),
            pl.BlockSpec((bb, GLA_STEP, GLA_KEY), lambda n: (n, 0, 0)),
            pl.BlockSpec((bb, GLA_HEADS, GLA_DK, GLA_DV), lambda n: (n, 0, 0, 0)),
        ],
        out_specs=[
            pl.BlockSpec((bb, GLA_STEP, GLA_WIDTH), lambda n: (n, 0, 0)),
            pl.BlockSpec((bb, GLA_HEADS, GLA_DK, GLA_DV), lambda n: (n, 0, 0, 0)),
        ],
        out_shape=[
            jax.ShapeDtypeStruct((db, GLA_STEP, GLA_WIDTH), F32),
            jax.ShapeDtypeStruct((db, GLA_HEADS, GLA_DK, GLA_DV), F32),
        ],
        compiler_params=pltpu.CompilerParams(
            dimension_semantics=("arbitrary",), vmem_limit_bytes=VMEM_LIMIT),
        name="gla_sample",
    )(gq, gk, gv, gb, state)


def _head_mask():
    head_of_col = lax.broadcasted_iota(jnp.int32, (SB_HEADS, SB_WIDTH), 1) // SB_HD
    return head_of_col == lax.broadcasted_iota(jnp.int32, (SB_HEADS, SB_WIDTH), 0)


def _sb_sample_kernel(pt_ref, q_ref, kn_ref, vn_ref, *refs, dec_seq, npg):
    del pt_ref
    k_refs = refs[:npg]
    v_refs = refs[npg:2 * npg]
    u2_ref, bias_ref, spread_ref, o_ref, k2d_scr = refs[2 * npg:]
    n_rows = dec_seq * SB_HEADS
    u2 = u2_ref[...]
    zbias2 = bias_ref[...][:, :1] * LOG2E

    hmask = _head_mask()
    qbd = jnp.concatenate(
        [jnp.where(hmask, jnp.broadcast_to(q_ref[0, t:t + 1, :], (SB_HEADS, SB_WIDTH)), 0.0)
         for t in range(dec_seq)], axis=0)

    lane = lax.broadcasted_iota(jnp.int32, (n_rows, LANE), 1)
    t_of_row = lax.broadcasted_iota(jnp.int32, (n_rows, LANE), 0) // SB_HEADS
    z_new = jnp.zeros((n_rows, LANE), F32)
    for t in range(dec_seq):
        zc = jnp.sum(qbd * kn_ref[0, t:t + 1, :], axis=-1, keepdims=True)
        z_new = jnp.where(lane == t, zc, z_new)
    mask = lane < t_of_row
    a_new, carry = _sb_weights(_sb_scores(z_new, zbias2, mask, u2), mask, jnp.zeros((n_rows, 1), F32))
    a_new = a_new.astype(F32)
    acc = jnp.zeros((n_rows, SB_HD), F32)
    for t in range(dec_seq):
        v_heads = jnp.concatenate(
            [vn_ref[0, t:t + 1, h * SB_HD:(h + 1) * SB_HD] for h in range(SB_HEADS)], axis=0)
        acc = acc + a_new[:, t:t + 1] * jnp.concatenate([v_heads] * dec_seq, axis=0)

    for s in range(npg):
        for h in range(SB_HEADS):
            rows = slice(s * PAGE_SIZE, (s + 1) * PAGE_SIZE)
            cols = slice(h * SB_HD, (h + 1) * SB_HD)
            k2d_scr[rows, cols] = k_refs[s][pl.ds(h, PAGE_SIZE, stride=SB_HEADS), :].astype(BF16)
    scores = _sb_scores(_dot_nt(qbd.astype(BF16), k2d_scr[...]), zbias2, None, u2)
    a, _ = _sb_weights(scores, None, carry)

    a_pages = jnp.concatenate([a[:, p * PAGE_SIZE:(p + 1) * PAGE_SIZE] for p in range(npg)], axis=0)
    a_spread = _dot(a_pages, spread_ref[...])
    flat = (n_rows, PAGE_SIZE * SB_HEADS)
    own_head = (lax.broadcasted_iota(jnp.int32, flat, 1) % SB_HEADS
                == lax.broadcasted_iota(jnp.int32, flat, 0) % SB_HEADS)
    for p in range(npg):
        a_p = jnp.where(own_head, a_spread[p * n_rows:(p + 1) * n_rows], 0.0).astype(BF16)
        acc = acc + _dot(a_p, v_refs[p][...].astype(BF16))

    for t in range(dec_seq):
        o_ref[0, t:t + 1, :] = jnp.concatenate(
            [acc[t * SB_HEADS + h:t * SB_HEADS + h + 1, :] for h in range(SB_HEADS)], axis=1)


def _sb_sample(page_table, q, k_new, v_new, cache_k, cache_v, u2, bias_rows):
    db, dec_seq, _ = q.shape
    npg = page_table.shape[1]
    n_rows = dec_seq * SB_HEADS
    page_rows = PAGE_SIZE * SB_HEADS
    pt_flat = page_table.reshape(-1)
    spread = (jnp.arange(page_rows)[None, :] // SB_HEADS == jnp.arange(PAGE_SIZE)[:, None]).astype(BF16)

    def page_spec(s):
        return pl.BlockSpec((page_rows, SB_HD), lambda b, pt: (pt[b * npg + s], 0))

    small = pl.BlockSpec((1, dec_seq, SB_WIDTH), lambda b, pt: (b, 0, 0))
    grid_spec = pltpu.PrefetchScalarGridSpec(
        num_scalar_prefetch=1,
        grid=(db,),
        in_specs=[small, small, small]
        + [page_spec(s) for s in range(npg)]
        + [page_spec(s) for s in range(npg)]
        + [pl.BlockSpec((LANE, LANE), lambda b, pt: (0, 0)),
           pl.BlockSpec((n_rows, LANE), lambda b, pt: (0, 0)),
           pl.BlockSpec((PAGE_SIZE, page_rows), lambda b, pt: (0, 0))],
        out_specs=pl.BlockSpec((1, dec_seq, SB_WIDTH), lambda b, pt: (b, 0, 0)),
        scratch_shapes=[pltpu.VMEM((npg * PAGE_SIZE, SB_WIDTH), BF16)],
    )
    kern = functools.partial(_sb_sample_kernel, dec_seq=dec_seq, npg=npg)
    return pl.pallas_call(
        kern,
        grid_spec=grid_spec,
        out_shape=jax.ShapeDtypeStruct((db, dec_seq, SB_WIDTH), F32),
        compiler_params=pltpu.CompilerParams(
            dimension_semantics=("arbitrary",), vmem_limit_bytes=VMEM_LIMIT),
        name="sb_sample",
    )(pt_flat, q, k_new, v_new, *([cache_k] * npg), *([cache_v] * npg), u2, bias_rows, spread)


def _out_proj_kernel(osb_ref, sg_ref, ogla_ref, gg_ref, x_ref, gng_ref, w_ref, png_ref, y_ref):
    m1 = osb_ref[...] * jax.nn.silu(sg_ref[...].astype(F32))
    og = ogla_ref[...]
    gng = gng_ref[...]
    normed = []
    for h in range(GLA_HEADS):
        blk = og[:, h * GLA_DV:(h + 1) * GLA_DV]
        var = jnp.mean(blk * blk, axis=-1, keepdims=True)
        normed.append(blk * lax.rsqrt(var + EPS) * gng)
    m2 = jnp.concatenate(normed, axis=1) * jax.nn.silu(gg_ref[...].astype(F32))
    mixed = jnp.concatenate([m1, m2], axis=1).astype(BF16)
    br = _dot(mixed, w_ref[...])
    var = jnp.mean(br * br, axis=-1, keepdims=True)
    y_ref[...] = x_ref[...] + br * lax.rsqrt(var + EPS) * png_ref[...]


def _out_proj(o_sb, u, o_gla, x, gng, w_out, png, *, tm):
    m, d = x.shape
    assert m % tm == 0
    sgb = COL_SG // SB_WIDTH
    ggb = COL_GG // GLA_WIDTH
    mix = SB_WIDTH + GLA_WIDTH
    return pl.pallas_call(
        _out_proj_kernel,
        grid=(m // tm,),
        in_specs=[
            pl.BlockSpec((tm, SB_WIDTH), lambda i: (i, 0)),
            pl.BlockSpec((tm, SB_WIDTH), lambda i: (i, sgb)),
            pl.BlockSpec((tm, GLA_WIDTH), lambda i: (i, 0)),
            pl.BlockSpec((tm, GLA_WIDTH), lambda i: (i, ggb)),
            pl.BlockSpec((tm, d), lambda i: (i, 0)),
            pl.BlockSpec((1, GLA_DV), lambda i: (0, 0)),
            pl.BlockSpec((mix, d), lambda i: (0, 0), pipeline_mode=pl.Buffered(1)),
            pl.BlockSpec((1, d), lambda i: (0, 0)),
        ],
        out_specs=pl.BlockSpec((tm, d), lambda i: (i, 0)),
        out_shape=jax.ShapeDtypeStruct((m, d), F32),
        compiler_params=pltpu.CompilerParams(
            dimension_semantics=("arbitrary",), vmem_limit_bytes=VMEM_LIMIT),
        name="out_proj",
    )(o_sb, u, o_gla, u, x, gng, w_out, png)


def _tri(n):
    return (jnp.arange(n)[None, :] < jnp.arange(n)[:, None]).astype(BF16)


def _seg_tri(block):
    r = jnp.arange(LANE)[:, None]
    c = jnp.arange(LANE)[None, :]
    return ((c <= r) & (c // block == r // block)).astype(BF16)


def _largest_tile(m, cap, quantum=SUBLANE):
    for t in range(min(cap, m) // quantum * quantum, 0, -quantum):
        if m % t == 0:
            return t
    raise ValueError(f"no {quantum}-aligned tile divides {m}")


def kernel(x_prompt, x_sample, cache_k, cache_v, state_gla, page_table, meta_tokens,
           norm_pre_g, w_in, sb_bias, w_alpha, b_alpha, gla_norm_g, w_out, norm_post_g):
    n_batch, seq, d = x_prompt.shape
    db, dec_seq, _ = x_sample.shape
    assert w_in.shape[0] == 1 and GLA_STEP % dec_seq == 0 and seq % GLA_STEP == 0
    n_prompt = n_batch * seq
    n_sample = db * dec_seq
    xp = x_prompt.reshape(n_prompt, d)
    assert n_sample % LANE == 0
    xs = jnp.concatenate([x_sample.reshape(n_sample, d), meta_tokens,
                          jnp.zeros((LANE - N_META, d), F32)], axis=0)
    meta_row = n_sample

    w_main = w_in[0].astype(BF16)
    w_ga = jnp.pad(w_in[0, :, N_MAIN:], ((0, 0), (0, LANE - GLA_RANK))).astype(BF16)
    w_al = jnp.pad(w_alpha[0], ((0, LANE - GLA_RANK), (0, 0))).astype(BF16)
    proj = functools.partial(_in_proj, g_pre=norm_pre_g, w_main=w_main, w_ga=w_ga, w_al=w_al,
                             b_al=b_alpha, tn=SB_WIDTH)
    step_seg = _seg_tri(GLA_STEP)
    u_p, b_p, k_flat, v_flat = proj(xp, seg=jnp.stack([step_seg, step_seg]), u_dtype=BF16,
                                    tm=_largest_tile(seq, 1024, LANE), cache_rows=(seq, N_META))
    u_s, b_s = proj(xs, seg=jnp.stack([_seg_tri(dec_seq), step_seg]), u_dtype=F32,
                    tm=n_sample + LANE, n_first=n_sample // LANE)

    w_out_bf = w_out[0].astype(BF16)

    tq = 256
    u2 = _tri(LANE)
    o_sb_p = _sb_prompt(u_p, u_s, sb_bias[0], u2, n_batch=n_batch, seq=seq, tq=tq, hp=4,
                        meta_row=meta_row)
    o_gla_p, s_end = _gla_prompt(u_p.reshape(n_batch, seq, N_MAIN), b_p.reshape(n_batch, seq, GLA_KEY),
                                 u_s, b_s, n_batch=n_batch, seq=seq, tb=_largest_tile(seq, 128),
                                 bb=_largest_tile(n_batch, 8, 1), meta_row=meta_row)
    y_prompt = _out_proj(o_sb_p, u_p, o_gla_p.reshape(n_prompt, GLA_WIDTH), xp, gla_norm_g, w_out_bf,
                         norm_post_g, tm=_largest_tile(n_prompt, 256)).reshape(n_batch, seq, d)
    k_flat, v_flat = _meta_fill(k_flat, v_flat, u_s, n_seq=n_batch, meta_row=meta_row)
    cache_shape = (1, n_batch, N_META + seq, SB_HEADS, SB_HD)
    new_k_prompt = k_flat.reshape(cache_shape)
    new_v_prompt = v_flat.reshape(cache_shape)

    sample_cols = lambda t, col, width: t[:n_sample, col:col + width].reshape(db, dec_seq, width)
    q_s = sample_cols(u_s, COL_SQ, SB_WIDTH)
    k_s = sample_cols(u_s, COL_SK, SB_WIDTH)
    v_s = sample_cols(u_s, COL_SV, SB_WIDTH)
    n_pool = cache_k.shape[1]
    ck = cache_k.reshape(n_pool * PAGE_SIZE * SB_HEADS, SB_HD)
    cv = cache_v.reshape(n_pool * PAGE_SIZE * SB_HEADS, SB_HD)
    bias_rows = jnp.broadcast_to(jnp.tile(sb_bias[0], dec_seq)[:, None], (dec_seq * SB_HEADS, LANE))
    o_sb_s = _sb_sample(page_table, q_s, k_s, v_s, ck, cv, u2, bias_rows)

    pad_rows = lambda t: jnp.pad(t, ((0, 0), (0, GLA_STEP - dec_seq), (0, 0)))
    o_gla_s, s_new = _gla_sample(
        pad_rows(sample_cols(u_s, COL_GQ, GLA_KEY)), pad_rows(sample_cols(u_s, COL_GK, GLA_KEY)),
        pad_rows(sample_cols(u_s, COL_GV, GLA_WIDTH)), pad_rows(sample_cols(b_s, 0, GLA_KEY)),
        state_gla[0], bb=_largest_tile(db, 8, 1), dec_seq=dec_seq)
    y_sample = _out_proj(o_sb_s.reshape(n_sample, SB_WIDTH), u_s,
                         o_gla_s[:, :dec_seq].reshape(n_sample, GLA_WIDTH), x_sample.reshape(n_sample, d),
                         gla_norm_g, w_out_bf, norm_post_g,
                         tm=_largest_tile(n_sample, 256)).reshape(db, dec_seq, d)

    new_k_sample = k_s.reshape(1, db, dec_seq, SB_HEADS, SB_HD)
    new_v_sample = v_s.reshape(1, db, dec_seq, SB_HEADS, SB_HD)
    return (y_prompt, y_sample, new_k_prompt, new_v_prompt, s_end[None],
            new_k_sample, new_v_sample, s_new[None])
```

```python
import functools
import math

import jax
import jax.numpy as jnp
from jax import lax
from jax.experimental import pallas as pl
from jax.experimental.pallas import tpu as pltpu

F32 = jnp.float32
BF16 = jnp.bfloat16

N_META = 16
SB_HEADS = 8
SB_HD = 128
GLA_HEADS = 4
GLA_DK = 128
GLA_DV = 256
GLA_RANK = 16
GLA_TAU = 16.0
PAGE_SIZE = 128
EPS = 1e-6

LANE = 128
SUBLANE = 8
VMEM_LIMIT = 56 * 1024 * 1024
LOG2E = math.log2(math.e)

SB_WIDTH = SB_HEADS * SB_HD
GLA_KEY = GLA_HEADS * GLA_DK
GLA_WIDTH = GLA_HEADS * GLA_DV
N_MAIN = 4 * SB_WIDTH + 2 * GLA_KEY + 2 * GLA_WIDTH
COL_SQ, COL_SK, COL_SV, COL_SG = 0, SB_WIDTH, 2 * SB_WIDTH, 3 * SB_WIDTH
COL_GQ = 4 * SB_WIDTH
COL_GK = COL_GQ + GLA_KEY
COL_GV = COL_GK + GLA_KEY
COL_GG = COL_GV + GLA_WIDTH

GLA_STEP = 2 * SUBLANE
assert N_META == GLA_STEP


def _log_sigmoid(z):
    return jnp.minimum(z, 0.0) - jnp.log(1.0 + jnp.exp(-jnp.abs(z)))


def _split_bf16(x):
    hi = x.astype(BF16)
    lo = (x - hi.astype(F32)).astype(BF16)
    return hi, lo


def _dot(a, b):
    return jnp.dot(a, b, preferred_element_type=F32)


def _dot_nt(a, b):
    return lax.dot_general(a, b, (((1,), (1,)), ((), ())), preferred_element_type=F32)


def _dot_tn(a, b):
    return lax.dot_general(a, b, (((0,), (0,)), ((), ())), preferred_element_type=F32)


def _in_proj_kernel(x_ref, g_ref, w_ref, wga_ref, wal_ref, bal_ref, seg_ref, u_ref, lf_ref, *rest,
                    n_first, cache_cols):
    h_scr = rest[-1]
    j = pl.program_id(1)

    @pl.when(j == 0)
    def _():
        x = x_ref[...]
        var = jnp.mean(x * x, axis=-1, keepdims=True)
        h = (x * lax.rsqrt(var + EPS) * g_ref[...]).astype(BF16)
        h_scr[...] = h
        ga = _dot(h, wga_ref[...])
        pre = _dot(ga.astype(BF16), wal_ref[...]) + bal_ref[...]
        g_hi, g_lo = _split_bf16(_log_sigmoid(pre) / GLA_TAU)
        for r in range(x.shape[0] // LANE):
            rows = slice(r * LANE, (r + 1) * LANE)
            seg = seg_ref[0 if r < n_first else 1]
            lf_ref[rows, :] = _dot(seg, g_hi[rows]) + _dot(seg, g_lo[rows])

    res = _dot(h_scr[...], w_ref[...])
    u_ref[...] = res.astype(u_ref.dtype)
    for ref, col in zip(rest[:-1], cache_cols):
        @pl.when(j == col)
        def _(ref=ref):
            for h in range(SB_HEADS):
                ref[pl.ds(h, res.shape[0], stride=SB_HEADS), :] = res[:, h * SB_HD:(h + 1) * SB_HD]


def _in_proj(x, g_pre, w_main, w_ga, w_al, b_al, seg, *, tm, tn, u_dtype, n_first=None, cache_rows=None):
    m, d = x.shape
    n = N_MAIN
    assert m % tm == 0 and n % tn == 0 and tm % LANE == 0 and w_main.shape[1] >= n
    n_first = tm // LANE if n_first is None else n_first
    out_specs = [
        pl.BlockSpec((tm, tn), lambda i, j: (i, j)),
        pl.BlockSpec((tm, GLA_KEY), lambda i, j: (i, 0)),
    ]
    out_shape = [
        jax.ShapeDtypeStruct((m, n), u_dtype),
        jax.ShapeDtypeStruct((m, GLA_KEY), F32),
    ]
    cache_cols = ()
    if cache_rows is not None:
        seq, lead = cache_rows
        assert tn == SB_WIDTH and seq % tm == 0 and m % seq == 0
        per_seq = seq // tm
        cache_cols = (COL_SK // tn, COL_SV // tn)

        def cache_index(i, j):
            return (((i // per_seq) * (seq + lead) + lead + (i % per_seq) * tm) * SB_HEADS, 0)

        flat = jax.ShapeDtypeStruct(((m // seq) * (seq + lead) * SB_HEADS, SB_HD), F32)
        for _ in cache_cols:
            out_specs.append(pl.BlockSpec((pl.Element(tm * SB_HEADS), pl.Element(SB_HD)), cache_index,
                                          pipeline_mode=pl.Buffered(1)))
            out_shape.append(flat)
    kern = functools.partial(_in_proj_kernel, n_first=n_first, cache_cols=cache_cols)
    return pl.pallas_call(
        kern,
        grid=(m // tm, n // tn),
        in_specs=[
            pl.BlockSpec((tm, d), lambda i, j: (i, 0)),
            pl.BlockSpec((1, d), lambda i, j: (0, 0)),
            pl.BlockSpec((d, tn), lambda i, j: (0, j)),
            pl.BlockSpec((d, LANE), lambda i, j: (0, 0)),
            pl.BlockSpec((LANE, GLA_KEY), lambda i, j: (0, 0)),
            pl.BlockSpec((1, GLA_KEY), lambda i, j: (0, 0)),
            pl.BlockSpec((2, LANE, LANE), lambda i, j: (0, 0, 0)),
        ],
        out_specs=out_specs,
        out_shape=out_shape,
        scratch_shapes=[pltpu.VMEM((tm, d), BF16)],
        compiler_params=pltpu.CompilerParams(
            dimension_semantics=("arbitrary", "arbitrary"), vmem_limit_bytes=VMEM_LIMIT),
        name="in_proj",
    )(x, g_pre, w_main, w_ga, w_al, b_al, seg)


def _meta_fill_kernel(k_any, v_any, mk_ref, mv_ref, ko_ref, vo_ref):
    del k_any, v_any
    for ref, src in ((ko_ref, mk_ref), (vo_ref, mv_ref)):
        for h in range(SB_HEADS):
            ref[pl.ds(h, N_META, stride=SB_HEADS), :] = src[:N_META, h * SB_HD:(h + 1) * SB_HD]


def _meta_fill(k_flat, v_flat, u_m, *, n_seq, meta_row):
    rows = N_META * SB_HEADS
    per_seq = k_flat.shape[0] // n_seq
    assert per_seq % rows == 0
    blocks = per_seq // rows
    out = pl.BlockSpec((rows, SB_HD), lambda b: (b * blocks, 0))
    return pl.pallas_call(
        _meta_fill_kernel,
        grid=(n_seq,),
        in_specs=[
            pl.BlockSpec(memory_space=pl.ANY),
            pl.BlockSpec(memory_space=pl.ANY),
            pl.BlockSpec((LANE, SB_WIDTH), lambda b: (meta_row // LANE, COL_SK // SB_WIDTH)),
            pl.BlockSpec((LANE, SB_WIDTH), lambda b: (meta_row // LANE, COL_SV // SB_WIDTH)),
        ],
        out_specs=[out, out],
        out_shape=[jax.ShapeDtypeStruct(k_flat.shape, F32), jax.ShapeDtypeStruct(v_flat.shape, F32)],
        input_output_aliases={0: 0, 1: 1},
        compiler_params=pltpu.CompilerParams(dimension_semantics=("arbitrary",)),
        name="meta_fill",
    )(k_flat, v_flat, u_m, u_m)


SB_ZSCALE2 = (SB_HD ** -0.5) * LOG2E
SB_LAG = 2
SB_AHEAD = 3


def _softplus2(z_raw, zbias2):
    z2 = z_raw * SB_ZSCALE2 + zbias2
    t2 = jnp.log(1.0 + jnp.exp2(-jnp.abs(z2))) * LOG2E
    sp2 = jnp.maximum(z2, 0.0) + t2
    return sp2, z2 - sp2


def _tri_cumsum(sp2, u2):
    return _dot(sp2.astype(BF16), u2)


def _sb_scores(z_raw, zbias2, mask, u2):
    sp2, ls2 = _softplus2(z_raw, zbias2)
    if mask is not None:
        sp2 = jnp.where(mask, sp2, 0.0)
    subs = [sp2[:, j * LANE:(j + 1) * LANE] for j in range(sp2.shape[1] // LANE)]
    loc = _tri_cumsum(jnp.concatenate(subs, axis=0), u2)
    return ls2, loc, [jnp.sum(s, axis=-1, keepdims=True) for s in subs]


def _sb_weights(scores, mask, carry):
    ls2, loc, sums = scores
    rows = ls2.shape[0]
    later = [None] * len(sums)
    for j in range(len(sums) - 1, -1, -1):
        later[j] = loc[j * rows:(j + 1) * rows] + carry
        carry = carry + sums[j]
    a = jnp.exp2(ls2 - jnp.concatenate(later, axis=1))
    if mask is not None:
        a = jnp.where(mask, a, 0.0)
    return a.astype(BF16), carry


def _sb_streams(streams, u2):
    n_blocks = len(streams[0][2])
    order = [(s, b) for b in range(n_blocks) for s in range(len(streams))]
    qk = lambda s, b: _dot_nt(streams[s][0], streams[s][2][b][0])
    zs = {sb: qk(*sb) for sb in order[:SB_AHEAD]}
    acc = [st[3] for st in streams]
    carry = [st[4] for st in streams]
    scores = {}
    for n in range(len(order) + SB_LAG):
        if n + SB_AHEAD < len(order):
            zs[order[n + SB_AHEAD]] = qk(*order[n + SB_AHEAD])
        if n < len(order):
            s, b = order[n]
            scores[s, b] = _sb_scores(zs.pop((s, b)), streams[s][1], streams[s][2][b][2], u2)
        if n >= SB_LAG:
            s, b = order[n - SB_LAG]
            _, v, m = streams[s][2][b]
            a, carry[s] = _sb_weights(scores.pop((s, b)), m, carry[s])
            acc[s] = acc[s] + _dot(a, v)
    return list(zip(acc, carry))


def _sb_prompt_kernel(bias_ref, q_ref, k_ref, v_ref, km_ref, vm_ref, u2_ref, o_ref, *, tq, hp):
    hb = pl.program_id(1)
    i = pl.program_id(2)
    u2 = u2_ref[...]
    cols = [slice(j * SB_HD, (j + 1) * SB_HD) for j in range(hp)]
    zb = [bias_ref[hb * hp + j] * LOG2E for j in range(hp)]
    q = [q_ref[:, c].astype(BF16) for c in cols]

    def kv(chunk, c):
        rows = pl.ds(pl.multiple_of(chunk * tq, tq), tq)
        return k_ref[rows, c].astype(BF16), v_ref[rows, c].astype(BF16)

    zero_carry = jnp.zeros((tq, 1), F32)
    mmask = lax.broadcasted_iota(jnp.int32, (tq, LANE), 1) < N_META
    z_meta = [_dot_nt(q[j], km_ref[:, cols[j]].astype(BF16)) for j in range(hp)]
    a_meta = [_sb_weights(_sb_scores(z_meta[j], zb[j], mmask, u2), mmask, zero_carry)[0]
              for j in range(hp)]
    pv_meta = [_dot(a_meta[j], vm_ref[:, cols[j]].astype(BF16)) for j in range(hp)]

    tri = lax.broadcasted_iota(jnp.int32, (tq, tq), 1) < lax.broadcasted_iota(jnp.int32, (tq, tq), 0)

    def pair(p, state):
        c0 = i - 2 * p
        first = jnp.logical_or(tri, p > 0)
        return tuple(_sb_streams(
            [(q[j], zb[j], [kv(c0, cols[j]) + (first,), kv(c0 - 1, cols[j]) + (None,)], *state[j])
             for j in range(hp)], u2))

    def single(_, state):
        only = jnp.logical_or(tri, i > 0)
        return tuple(_sb_streams(
            [(q[j], zb[j], [kv(0, cols[j]) + (only,)], *state[j]) for j in range(hp)], u2))

    state = tuple((jnp.zeros((tq, SB_HD), F32), zero_carry) for _ in range(hp))
    state = lax.fori_loop(0, lax.shift_right_logical(i + 1, 1), pair, state)
    state = lax.fori_loop(0, lax.bitwise_and(i + 1, 1), single, state)

    for j in range(hp):
        acc, carry = state[j]
        o_ref[:, cols[j]] = acc + pv_meta[j] * jnp.exp2(-carry)


def _sb_prompt(u_p, u_m, sb_bias, u2, *, n_batch, seq, tq, hp, meta_row):
    nq = seq // tq
    w = hp * SB_HD
    qb = COL_SQ // w
    kb = COL_SK // w
    vb = COL_SV // w
    kern = functools.partial(_sb_prompt_kernel, tq=tq, hp=hp)
    return pl.pallas_call(
        kern,
        grid=(n_batch, SB_HEADS // hp, nq),
        in_specs=[
            pl.BlockSpec(memory_space=pltpu.SMEM),
            pl.BlockSpec((tq, w), lambda b, h, i: (b * nq + i, qb + h)),
            pl.BlockSpec((seq, w), lambda b, h, i: (b, kb + h)),
            pl.BlockSpec((seq, w), lambda b, h, i: (b, vb + h)),
            pl.BlockSpec((LANE, w), lambda b, h, i: (meta_row // LANE, kb + h)),
            pl.BlockSpec((LANE, w), lambda b, h, i: (meta_row // LANE, vb + h)),
            pl.BlockSpec((LANE, LANE), lambda b, h, i: (0, 0)),
        ],
        out_specs=pl.BlockSpec((tq, w), lambda b, h, i: (b * nq + i, h)),
        out_shape=jax.ShapeDtypeStruct((n_batch * seq, SB_WIDTH), F32),
        compiler_params=pltpu.CompilerParams(
            dimension_semantics=("arbitrary", "arbitrary", "arbitrary"), vmem_limit_bytes=VMEM_LIMIT),
        name="sb_prompt",
    )(sb_bias, u_p, u_p, u_p, u_m, u_m, u2)


def _gla_steps(chains, t_last):
    t_len = chains[0][1].shape[0]
    lane = lax.broadcasted_iota(jnp.int32, (t_len, t_len), 1)
    row = lax.broadcasted_iota(jnp.int32, (t_len, t_len), 0)
    half = SUBLANE
    assert t_len == 2 * half
    prep = []
    for st_prev, q, k, v, b in chains:
        k32 = k.astype(F32)
        b_last = b[t_last:t_last + 1, :]
        pieces = []
        for s in range(t_len):
            rows = slice(s // half * half, s // half * half + half)
            pieces.append(q[rows] * jnp.exp(jnp.minimum(b[rows] - b[s:s + 1, :], 0.0)))
        lhs = jnp.concatenate(pieces, axis=0).astype(BF16)
        b_mid = b[half - 1:half, :]
        q_late = (q[half:] * jnp.exp(jnp.minimum(b[half:] - b_mid, 0.0))).astype(BF16)
        k_early = k32[:half] * jnp.exp(jnp.minimum(b_mid - b[:half], 0.0))
        k_early = jnp.concatenate([k_early, jnp.zeros_like(k_early)], axis=0).astype(BF16)
        qe = (q * jnp.exp(b)).astype(BF16)
        kd = (k32 * jnp.exp(b_last - b)).astype(BF16)
        prep.append((lhs, q_late, k_early, qe, kd, k.astype(BF16), v.astype(BF16), jnp.exp(b_last)))
    mm = [(_dot_nt(lhs, k_bf), _dot_nt(q_late, k_early), _dot_nt(qe, st_prev.astype(BF16)),
           _dot_tn(v_bf, kd))
          for (lhs, q_late, k_early, qe, kd, k_bf, v_bf, _), (st_prev, *_) in zip(prep, chains)]
    lane_h = lane[:half]
    attns = []
    for r, cross, _, _ in mm:
        top = jnp.zeros((half, t_len), F32)
        for s in range(half):
            top = jnp.where(lane_h == s, r[s * half:(s + 1) * half, :], top)
        bottom = cross
        for s in range(half, t_len):
            bottom = jnp.where(lane_h == s, r[s * half:(s + 1) * half, :], bottom)
        attn = jnp.concatenate([top, bottom], axis=0)
        attns.append(jnp.where(lane <= row, attn, 0.0).astype(BF16))
    out = []
    for attn, (_, _, o_state, inc), p, (st_prev, *_) in zip(attns, mm, prep, chains):
        out.append((st_prev * p[7] + inc, _dot(attn, p[6]) + o_state))
    return out


def _head_cols(h):
    return slice(h * GLA_DK, (h + 1) * GLA_DK), slice(h * GLA_DV, (h + 1) * GLA_DV)


def _gla_prompt_kernel(q_ref, k_ref, v_ref, b_ref, qm_ref, km_ref, vm_ref, bm_ref,
                       o_ref, s_ref, st_scr, *, n_steps, bb):
    blk = pl.program_id(1)
    scale = GLA_DK ** -0.5

    @pl.when(blk == 0)
    def _():
        chains = []
        for h in range(GLA_HEADS):
            kc, vc = _head_cols(h)
            chains.append((jnp.zeros((GLA_DV, GLA_DK), F32), qm_ref[:, kc].astype(F32) * scale, km_ref[:, kc],
                           vm_ref[:, vc], bm_ref[:, kc]))
        for h, (st1, _) in enumerate(_gla_steps(chains, GLA_STEP - 1)):
            for n in range(bb):
                st_scr[n, h] = st1

    def body(step, _):
        rows = pl.ds(pl.multiple_of(step * GLA_STEP, GLA_STEP), GLA_STEP)
        chains = []
        for n in range(bb):
            for h in range(GLA_HEADS):
                kc, vc = _head_cols(h)
                chains.append((st_scr[n, h], q_ref[n, rows, kc].astype(F32) * scale, k_ref[n, rows, kc],
                               v_ref[n, rows, vc], b_ref[n, rows, kc]))
        res = _gla_steps(chains, GLA_STEP - 1)
        for n in range(bb):
            for h in range(GLA_HEADS):
                st_scr[n, h] = res[n * GLA_HEADS + h][0]
            o_ref[n, rows, :] = jnp.concatenate(
                [res[n * GLA_HEADS + h][1] for h in range(GLA_HEADS)], axis=1)
        return 0

    lax.fori_loop(0, n_steps, body, 0)

    @pl.when(blk == pl.num_programs(1) - 1)
    def _():
        for n in range(bb):
            for h in range(GLA_HEADS):
                s_ref[n, h] = st_scr[n, h].T


def _gla_prompt(u_p, b_p, u_m, b_m, *, n_batch, seq, tb, bb, meta_row):
    qb = COL_GQ // GLA_KEY
    kb = COL_GK // GLA_KEY
    vb = COL_GV // GLA_WIDTH
    assert n_batch % bb == 0 and seq % tb == 0
    kern = functools.partial(_gla_prompt_kernel, n_steps=tb // GLA_STEP, bb=bb)
    return pl.pallas_call(
        kern,
        grid=(n_batch // bb, seq // tb),
        in_specs=[
            pl.BlockSpec((bb, tb, GLA_KEY), lambda n, j: (n, j, qb)),
            pl.BlockSpec((bb, tb, GLA_KEY), lambda n, j: (n, j, kb)),
            pl.BlockSpec((bb, tb, GLA_WIDTH), lambda n, j: (n, j, vb)),
            pl.BlockSpec((bb, tb, GLA_KEY), lambda n, j: (n, j, 0)),
            pl.BlockSpec((N_META, GLA_KEY), lambda n, j: (meta_row // N_META, qb)),
            pl.BlockSpec((N_META, GLA_KEY), lambda n, j: (meta_row // N_META, kb)),
            pl.BlockSpec((N_META, GLA_WIDTH), lambda n, j: (meta_row // N_META, vb)),
            pl.BlockSpec((N_META, GLA_KEY), lambda n, j: (meta_row // N_META, 0)),
        ],
        out_specs=[
            pl.BlockSpec((bb, tb, GLA_WIDTH), lambda n, j: (n, j, 0)),
            pl.BlockSpec((bb, GLA_HEADS, GLA_DK, GLA_DV), lambda n, j: (n, 0, 0, 0)),
        ],
        out_shape=[
            jax.ShapeDtypeStruct((n_batch, seq, GLA_WIDTH), F32),
            jax.ShapeDtypeStruct((n_batch, GLA_HEADS, GLA_DK, GLA_DV), F32),
        ],
        scratch_shapes=[pltpu.VMEM((bb, GLA_HEADS, GLA_DV, GLA_DK), F32)],
        compiler_params=pltpu.CompilerParams(
            dimension_semantics=("arbitrary", "arbitrary"), vmem_limit_bytes=VMEM_LIMIT),
        name="gla_prompt",
    )(u_p, u_p, u_p, b_p, u_m, u_m, u_m, b_m)


def _gla_sample_kernel(q_ref, k_ref, v_ref, b_ref, s_in_ref, o_ref, s_out_ref, *, bb, dec_seq):
    scale = GLA_DK ** -0.5
    chains = []
    for n in range(bb):
        for h in range(GLA_HEADS):
            kc, vc = _head_cols(h)
            chains.append((s_in_ref[n, h].T, q_ref[n, :, kc].astype(F32) * scale, k_ref[n, :, kc],
                           v_ref[n, :, vc], b_ref[n, :, kc]))
    res = _gla_steps(chains, dec_seq - 1)
    for n in range(bb):
        for h in range(GLA_HEADS):
            s_out_ref[n, h] = res[n * GLA_HEADS + h][0].T
        o_ref[n] = jnp.concatenate([res[n * GLA_HEADS + h][1] for h in range(GLA_HEADS)], axis=1)


def _gla_sample(gq, gk, gv, gb, state, *, bb, dec_seq):
    db = gq.shape[0]
    assert db % bb == 0
    kern = functools.partial(_gla_sample_kernel, bb=bb, dec_seq=dec_seq)
    return pl.pallas_call(
        kern,
        grid=(db // bb,),
        in_specs=[
            pl.BlockSpec((bb, GLA_STEP, GLA_KEY), lambda n: (n, 0, 0)),
            pl.BlockSpec((bb, GLA_STEP, GLA_KEY), lambda n: (n, 0, 0)),
            pl.BlockSpec((bb, GLA_STEP, GLA_WIDTH), lambda n: (n, 0, 0)),
            pl.BlockSpec((bb, GLA_STEP, GLA_KEY), lambda n: (n, 0, 0)),
            pl.BlockSpec((bb, GLA_HEADS, GLA_DK, GLA_DV), lambda n: (n, 0, 0, 0)),
        ],
        out_specs=[
            pl.BlockSpec((bb, GLA_STEP, GLA_WIDTH), lambda n: (n, 0, 0)),
            pl.BlockSpec((bb, GLA_HEADS, GLA_DK, GLA_DV), lambda n: (n, 0, 0, 0)),
        ],
        out_shape=[
            jax.ShapeDtypeStruct((db, GLA_STEP, GLA_WIDTH), F32),
            jax.ShapeDtypeStruct((db, GLA_HEADS, GLA_DK, GLA_DV), F32),
        ],
        compiler_params=pltpu.CompilerParams(
            dimension_semantics=("arbitrary",), vmem_limit_bytes=VMEM_LIMIT),
        name="gla_sample",
    )(gq, gk, gv, gb, state)


def _head_mask():
    head_of_col = lax.broadcasted_iota(jnp.int32, (SB_HEADS, SB_WIDTH), 1) // SB_HD
    return head_of_col == lax.broadcasted_iota(jnp.int32, (SB_HEADS, SB_WIDTH), 0)


def _sb_sample_kernel(pt_ref, q_ref, kn_ref, vn_ref, *refs, dec_seq, npg):
    del pt_ref
    k_refs = refs[:npg]
    v_refs = refs[npg:2 * npg]
    u2_ref, bias_ref, spread_ref, o_ref, k2d_scr = refs[2 * npg:]
    n_rows = dec_seq * SB_HEADS
    u2 = u2_ref[...]
    zbias2 = bias_ref[...][:, :1] * LOG2E

    hmask = _head_mask()
    qbd = jnp.concatenate(
        [jnp.where(hmask, jnp.broadcast_to(q_ref[0, t:t + 1, :], (SB_HEADS, SB_WIDTH)), 0.0)
         for t in range(dec_seq)], axis=0)

    lane = lax.broadcasted_iota(jnp.int32, (n_rows, LANE), 1)
    t_of_row = lax.broadcasted_iota(jnp.int32, (n_rows, LANE), 0) // SB_HEADS
    z_new = jnp.zeros((n_rows, LANE), F32)
    for t in range(dec_seq):
        zc = jnp.sum(qbd * kn_ref[0, t:t + 1, :], axis=-1, keepdims=True)
        z_new = jnp.where(lane == t, zc, z_new)
    mask = lane < t_of_row
    a_new, carry = _sb_weights(_sb_scores(z_new, zbias2, mask, u2), mask, jnp.zeros((n_rows, 1), F32))
    a_new = a_new.astype(F32)
    acc = jnp.zeros((n_rows, SB_HD), F32)
    for t in range(dec_seq):
        v_heads = jnp.concatenate(
            [vn_ref[0, t:t + 1, h * SB_HD:(h + 1) * SB_HD] for h in range(SB_HEADS)], axis=0)
        acc = acc + a_new[:, t:t + 1] * jnp.concatenate([v_heads] * dec_seq, axis=0)

    for s in range(npg):
        for h in range(SB_HEADS):
            rows = slice(s * PAGE_SIZE, (s + 1) * PAGE_SIZE)
            cols = slice(h * SB_HD, (h + 1) * SB_HD)
            k2d_scr[rows, cols] = k_refs[s][pl.ds(h, PAGE_SIZE, stride=SB_HEADS), :].astype(BF16)
    scores = _sb_scores(_dot_nt(qbd.astype(BF16), k2d_scr[...]), zbias2, None, u2)
    a, _ = _sb_weights(scores, None, carry)

    a_pages = jnp.concatenate([a[:, p * PAGE_SIZE:(p + 1) * PAGE_SIZE] for p in range(npg)], axis=0)
    a_spread = _dot(a_pages, spread_ref[...])
    flat = (n_rows, PAGE_SIZE * SB_HEADS)
    own_head = (lax.broadcasted_iota(jnp.int32, flat, 1) % SB_HEADS
                == lax.broadcasted_iota(jnp.int32, flat, 0) % SB_HEADS)
    for p in range(npg):
        a_p = jnp.where(own_head, a_spread[p * n_rows:(p + 1) * n_rows], 0.0).astype(BF16)
        acc = acc + _dot(a_p, v_refs[p][...].astype(BF16))

    for t in range(dec_seq):
        o_ref[0, t:t + 1, :] = jnp.concatenate(
            [acc[t * SB_HEADS + h:t * SB_HEADS + h + 1, :] for h in range(SB_HEADS)], axis=1)


def _sb_sample(page_table, q, k_new, v_new, cache_k, cache_v, u2, bias_rows):
    db, dec_seq, _ = q.shape
    npg = page_table.shape[1]
    n_rows = dec_seq * SB_HEADS
    page_rows = PAGE_SIZE * SB_HEADS
    pt_flat = page_table.reshape(-1)
    spread = (jnp.arange(page_rows)[None, :] // SB_HEADS == jnp.arange(PAGE_SIZE)[:, None]).astype(BF16)

    def page_spec(s):
        return pl.BlockSpec((page_rows, SB_HD), lambda b, pt: (pt[b * npg + s], 0))

    small = pl.BlockSpec((1, dec_seq, SB_WIDTH), lambda b, pt: (b, 0, 0))
    grid_spec = pltpu.PrefetchScalarGridSpec(
        num_scalar_prefetch=1,
        grid=(db,),
        in_specs=[small, small, small]
        + [page_spec(s) for s in range(npg)]
        + [page_spec(s) for s in range(npg)]
        + [pl.BlockSpec((LANE, LANE), lambda b, pt: (0, 0)),
           pl.BlockSpec((n_rows, LANE), lambda b, pt: (0, 0)),
           pl.BlockSpec((PAGE_SIZE, page_rows), lambda b, pt: (0, 0))],
        out_specs=pl.BlockSpec((1, dec_seq, SB_WIDTH), lambda b, pt: (b, 0, 0)),
        scratch_shapes=[pltpu.VMEM((npg * PAGE_SIZE, SB_WIDTH), BF16)],
    )
    kern = functools.partial(_sb_sample_kernel, dec_seq=dec_seq, npg=npg)
    return pl.pallas_call(
        kern,
        grid_spec=grid_spec,
        out_shape=jax.ShapeDtypeStruct((db, dec_seq, SB_WIDTH), F32),
        compiler_params=pltpu.CompilerParams(
            dimension_semantics=("arbitrary",), vmem_limit_bytes=VMEM_LIMIT),
        name="sb_sample",
    )(pt_flat, q, k_new, v_new, *([cache_k] * npg), *([cache_v] * npg), u2, bias_rows, spread)


def _out_proj_kernel(osb_ref, sg_ref, ogla_ref, gg_ref, x_ref, gng_ref, w_ref, png_ref, y_ref):
    m1 = osb_ref[...] * jax.nn.silu(sg_ref[...].astype(F32))
    og = ogla_ref[...]
    gng = gng_ref[...]
    normed = []
    for h in range(GLA_HEADS):
        blk = og[:, h * GLA_DV:(h + 1) * GLA_DV]
        var = jnp.mean(blk * blk, axis=-1, keepdims=True)
        normed.append(blk * lax.rsqrt(var + EPS) * gng)
    m2 = jnp.concatenate(normed, axis=1) * jax.nn.silu(gg_ref[...].astype(F32))
    mixed = jnp.concatenate([m1, m2], axis=1).astype(BF16)
    br = _dot(mixed, w_ref[...])
    var = jnp.mean(br * br, axis=-1, keepdims=True)
    y_ref[...] = x_ref[...] + br * lax.rsqrt(var + EPS) * png_ref[...]


def _out_proj(o_sb, u, o_gla, x, gng, w_out, png, *, tm):
    m, d = x.shape
    assert m % tm == 0
    sgb = COL_SG // SB_WIDTH
    ggb = COL_GG // GLA_WIDTH
    mix = SB_WIDTH + GLA_WIDTH
    return pl.pallas_call(
        _out_proj_kernel,
        grid=(m // tm,),
        in_specs=[
            pl.BlockSpec((tm, SB_WIDTH), lambda i: (i, 0)),
            pl.BlockSpec((tm, SB_WIDTH), lambda i: (i, sgb)),
            pl.BlockSpec((tm, GLA_WIDTH), lambda i: (i, 0)),
            pl.BlockSpec((tm, GLA_WIDTH), lambda i: (i, ggb)),
            pl.BlockSpec((tm, d), lambda i: (i, 0)),
            pl.BlockSpec((1, GLA_DV), lambda i: (0, 0)),
            pl.BlockSpec((mix, d), lambda i: (0, 0), pipeline_mode=pl.Buffered(1)),
            pl.BlockSpec((1, d), lambda i: (0, 0)),
        ],
        out_specs=pl.BlockSpec((tm, d), lambda i: (i, 0)),
        out_shape=jax.ShapeDtypeStruct((m, d), F32),
        compiler_params=pltpu.CompilerParams(
            dimension_semantics=("arbitrary",), vmem_limit_bytes=VMEM_LIMIT),
        name="out_proj",
    )(o_sb, u, o_gla, u, x, gng, w_out, png)


def _tri(n):
    return (jnp.arange(n)[None, :] < jnp.arange(n)[:, None]).astype(BF16)


def _seg_tri(block):
    r = jnp.arange(LANE)[:, None]
    c = jnp.arange(LANE)[None, :]
    return ((c <= r) & (c // block == r // block)).astype(BF16)


def _largest_tile(m, cap, quantum=SUBLANE):
    for t in range(min(cap, m) // quantum * quantum, 0, -quantum):
        if m % t == 0:
            return t
    raise ValueError(f"no {quantum}-aligned tile divides {m}")


def kernel(x_prompt, x_sample, cache_k, cache_v, state_gla, page_table, meta_tokens,
           norm_pre_g, w_in, sb_bias, w_alpha, b_alpha, gla_norm_g, w_out, norm_post_g):
    n_batch, seq, d = x_prompt.shape
    db, dec_seq, _ = x_sample.shape
    assert w_in.shape[0] == 1 and GLA_STEP % dec_seq == 0 and seq % GLA_STEP == 0
    n_prompt = n_batch * seq
    n_sample = db * dec_seq
    xp = x_prompt.reshape(n_prompt, d)
    assert n_sample % LANE == 0
    xs = jnp.concatenate([x_sample.reshape(n_sample, d), meta_tokens,
                          jnp.zeros((LANE - N_META, d), F32)], axis=0)
    meta_row = n_sample

    w_main = w_in[0].astype(BF16)
    w_ga = jnp.pad(w_in[0, :, N_MAIN:], ((0, 0), (0, LANE - GLA_RANK))).astype(BF16)
    w_al = jnp.pad(w_alpha[0], ((0, LANE - GLA_RANK), (0, 0))).astype(BF16)
    proj = functools.partial(_in_proj, g_pre=norm_pre_g, w_main=w_main, w_ga=w_ga, w_al=w_al,
                             b_al=b_alpha, tn=SB_WIDTH)
    step_seg = _seg_tri(GLA_STEP)
    u_p, b_p, k_flat, v_flat = proj(xp, seg=jnp.stack([step_seg, step_seg]), u_dtype=BF16,
                                    tm=_largest_tile(seq, 1024, LANE), cache_rows=(seq, N_META))
    u_s, b_s = proj(xs, seg=jnp.stack([_seg_tri(dec_seq), step_seg]), u_dtype=F32,
                    tm=n_sample + LANE, n_first=n_sample // LANE)

    w_out_bf = w_out[0].astype(BF16)

    tq = 256
    u2 = _tri(LANE)
    o_sb_p = _sb_prompt(u_p, u_s, sb_bias[0], u2, n_batch=n_batch, seq=seq, tq=tq, hp=4,
                        meta_row=meta_row)
    o_gla_p, s_end = _gla_prompt(u_p.reshape(n_batch, seq, N_MAIN), b_p.reshape(n_batch, seq, GLA_KEY),
                                 u_s, b_s, n_batch=n_batch, seq=seq, tb=_largest_tile(seq, 128),
                                 bb=_largest_tile(n_batch, 8, 1), meta_row=meta_row)
    y_prompt = _out_proj(o_sb_p, u_p, o_gla_p.reshape(n_prompt, GLA_WIDTH), xp, gla_norm_g, w_out_bf,
                         norm_post_g, tm=_largest_tile(n_prompt, 256)).reshape(n_batch, seq, d)
    k_flat, v_flat = _meta_fill(k_flat, v_flat, u_s, n_seq=n_batch, meta_row=meta_row)
    cache_shape = (1, n_batch, N_META + seq, SB_HEADS, SB_HD)
    new_k_prompt = k_flat.reshape(cache_shape)
    new_v_prompt = v_flat.reshape(cache_shape)

    sample_cols = lambda t, col, width: t[:n_sample, col:col + width].reshape(db, dec_seq, width)
    q_s = sample_cols(u_s, COL_SQ, SB_WIDTH)
    k_s = sample_cols(u_s, COL_SK, SB_WIDTH)
    v_s = sample_cols(u_s, COL_SV, SB_WIDTH)
    n_pool = cache_k.shape[1]
    ck = cache_k.reshape(n_pool * PAGE_SIZE * SB_HEADS, SB_HD)
    cv = cache_v.reshape(n_pool * PAGE_SIZE * SB_HEADS, SB_HD)
    bias_rows = jnp.broadcast_to(jnp.tile(sb_bias[0], dec_seq)[:, None], (dec_seq * SB_HEADS, LANE))
    o_sb_s = _sb_sample(page_table, q_s, k_s, v_s, ck, cv, u2, bias_rows)

    pad_rows = lambda t: jnp.pad(t, ((0, 0), (0, GLA_STEP - dec_seq), (0, 0)))
    o_gla_s, s_new = _gla_sample(
        pad_rows(sample_cols(u_s, COL_GQ, GLA_KEY)), pad_rows(sample_cols(u_s, COL_GK, GLA_KEY)),
        pad_rows(sample_cols(u_s, COL_GV, GLA_WIDTH)), pad_rows(sample_cols(b_s, 0, GLA_KEY)),
        state_gla[0], bb=_largest_tile(db, 8, 1), dec_seq=dec_seq)
    y_sample = _out_proj(o_sb_s.reshape(n_sample, SB_WIDTH), u_s,
                         o_gla_s[:, :dec_seq].reshape(n_sample, GLA_WIDTH), x_sample.reshape(n_sample, d),
                         gla_norm_g, w_out_bf, norm_post_g,
                         tm=_largest_tile(n_sample, 256)).reshape(db, dec_seq, d)

    new_k_sample = k_s.reshape(1, db, dec_seq, SB_HEADS, SB_HD)
    new_v_sample = v_s.reshape(1, db, dec_seq, SB_HEADS, SB_HD)
    return (y_prompt, y_sample, new_k_prompt, new_v_prompt, s_end[None],
            new_k_sample, new_v_sample, s_new[None])
```
